```python
import math
import jax, jax.numpy as jnp
from jax import lax
import numpy as np

D_MODEL = 1024
BATCH = 8
SEQ = 4096
DEPTH = 2
DEC_BATCH = 8
DEC_SEQ = 2048
PAST_LEN = 128

D_MIX = D_MODEL
LRU_WIDTH = D_MIX // 4
LRU_HEADS = 4
LRU_HEAD_DIM = LRU_WIDTH // LRU_HEADS
LRU_CONV = 4
LRU_C = 8.0
CONV_WIDTH = D_MIX // 4
CONV_KERNEL = 31
CONV_PAD = CONV_KERNEL // 2
ATT_WIDTH = D_MIX - LRU_WIDTH - CONV_WIDTH
ATT_HEADS = 4
ATT_VDIM = ATT_WIDTH // ATT_HEADS
ATT_QKDIM = ATT_VDIM // 2
ROPE_DIM = ATT_QKDIM // 4
ROPE_THETA = 500000.0
Q_BLOCK = 128
IN_COLS = 2 * LRU_WIDTH + 2 * CONV_WIDTH + 3 * ATT_WIDTH
N_GROUPS = 4
EXPERTS_PER_GROUP = 4
N_EXPERTS = N_GROUPS * EXPERTS_PER_GROUP
TOP_K_FINE = 2
D_EXPERT = D_MODEL // 2
EPS = 1e-6

kernel_name = 'hybrid_bidir_lru_conformer_diffattn_hmoe'


def rmsnorm(x, g):
    xf = x.astype(jnp.float32)
    y = xf * lax.rsqrt(jnp.mean(xf * xf, axis=-1, keepdims=True) + EPS)
    return (y * g.astype(jnp.float32)).astype(x.dtype)


def depthwise_conv(x, w, b, pad_lo, pad_hi):
    ch = x.shape[-1]
    y = lax.conv_general_dilated(x, w[:, None, :].astype(x.dtype), window_strides=(1,),
                                 padding=[(pad_lo, pad_hi)],
                                 dimension_numbers=('NWC', 'WIO', 'NWC'),
                                 feature_group_count=ch)
    return y + b.astype(x.dtype)


def linear_scan(a, b):
    def combine(left, right):
        return (left[0] * right[0], right[0] * left[1] + right[1])
    _, h = lax.associative_scan(combine, (a, b), axis=1)
    return h


def rglru_direction(x, wa, ba, wx, bx, lam):
    bsz, s, w = x.shape
    xh = x.reshape(bsz, s, LRU_HEADS, LRU_HEAD_DIM)
    r = jax.nn.sigmoid(jnp.einsum('bshi,hij->bshj', xh, wa.astype(jnp.float32)).reshape(bsz, s, w) + ba.astype(jnp.float32))
    i = jax.nn.sigmoid(jnp.einsum('bshi,hij->bshj', xh, wx.astype(jnp.float32)).reshape(bsz, s, w) + bx.astype(jnp.float32))
    log_a = -LRU_C * r * jax.nn.softplus(-lam.astype(jnp.float32))
    a = jnp.exp(log_a)
    mult = jnp.sqrt(-jnp.expm1(2.0 * log_a))
    return linear_scan(a, mult * (i * x))


def rope_partial(x, pos):
    half = ROPE_DIM // 2
    inv = ROPE_THETA ** (-(jnp.arange(half, dtype=jnp.float32) * 2.0) / ROPE_DIM)
    ang = pos.astype(jnp.float32)[:, None] * inv[None, :]
    cos = jnp.cos(ang)[None, :, None, None, :]
    sin = jnp.sin(ang)[None, :, None, None, :]
    xr = x[..., :ROPE_DIM].astype(jnp.float32)
    x1, x2 = xr[..., :half], xr[..., half:]
    rot = jnp.concatenate([x1 * cos - x2 * sin, x2 * cos + x1 * sin], axis=-1)
    return jnp.concatenate([rot.astype(x.dtype), x[..., ROPE_DIM:]], axis=-1)


def diff_attention(q, k, v, lam):
    bsz, s, nh, _, dq = q.shape
    nblk = s // Q_BLOCK
    scale = dq ** -0.5
    qb = q.reshape(bsz, nblk, Q_BLOCK, nh, 2, dq).transpose(1, 0, 2, 3, 4, 5)

    def block(qblk):
        sc = jnp.einsum('bqhcd,bkhcd->bhcqk', qblk, k, preferred_element_type=jnp.float32) * scale
        p = jax.nn.softmax(sc, axis=-1)
        wts = p[:, :, 0] - lam * p[:, :, 1]
        return jnp.einsum('bhqk,bkhe->bqhe', wts.astype(v.dtype), v)

    o = lax.map(block, qb)
    return o.transpose(1, 0, 2, 3, 4).reshape(bsz, s, nh, v.shape[-1])


def mixer(h, pos, lid, w_in, w_out, conv_w, conv_b, lru_wa, lru_ba, lru_wx, lru_bx, lru_lambda,
          dw_w, dw_b, cln_g, cln_b, lam_vec, subln_g):
    bsz, s, _ = h.shape
    proj = h @ w_in
    o1 = LRU_WIDTH
    o2 = 2 * LRU_WIDTH
    o3 = o2 + CONV_WIDTH
    o4 = o3 + CONV_WIDTH
    o5 = o4 + ATT_WIDTH
    o6 = o5 + ATT_WIDTH
    xa, ga, ub, gb, q, k, v = jnp.split(proj, [o1, o2, o3, o4, o5, o6], axis=-1)

    xa = depthwise_conv(xa, conv_w, conv_b, 2, 1).astype(jnp.float32)
    h_fwd = rglru_direction(xa, lru_wa[0], lru_ba[0], lru_wx[0], lru_bx[0], lru_lambda[0])
    h_bwd = jnp.flip(rglru_direction(jnp.flip(xa, axis=1), lru_wa[1], lru_ba[1], lru_wx[1], lru_bx[1], lru_lambda[1]), axis=1)
    ya = (jax.nn.gelu(ga.astype(jnp.float32)) * (h_fwd + h_bwd)).astype(h.dtype)

    ub = ub * jax.nn.sigmoid(gb)
    ub = depthwise_conv(ub, dw_w, dw_b, CONV_PAD, CONV_PAD).astype(jnp.float32)
    mu = jnp.mean(ub, axis=-1, keepdims=True)
    var = jnp.mean(jnp.square(ub - mu), axis=-1, keepdims=True)
    ub = (ub - mu) * lax.rsqrt(var + EPS) * cln_g.astype(jnp.float32) + cln_b.astype(jnp.float32)
    yb = jax.nn.silu(ub).astype(h.dtype)

    q = rope_partial(q.reshape(bsz, s, ATT_HEADS, 2, ATT_QKDIM), pos)
    k = rope_partial(k.reshape(bsz, s, ATT_HEADS, 2, ATT_QKDIM), pos)
    v = v.reshape(bsz, s, ATT_HEADS, ATT_VDIM)
    lam_init = 0.8 - 0.6 * math.exp(-0.3 * lid)
    lv = lam_vec.astype(jnp.float32)
    lam = jnp.exp(jnp.sum(lv[0] * lv[1])) - jnp.exp(jnp.sum(lv[2] * lv[3])) + lam_init
    o = diff_attention(q, k, v, lam)
    o = rmsnorm(o, subln_g) * (1.0 - lam_init)
    yc = o.reshape(bsz, s, ATT_WIDTH).astype(h.dtype)

    return jnp.concatenate([ya, yb, yc], axis=-1) @ w_out


def hier_moe(h, rw1, rb1, rw2, rb2, w1, w3, w2):
    bsz, s, d = h.shape
    t = h.reshape(bsz * s, d)
    n_tok = t.shape[0]
    gl = (t @ rw1).astype(jnp.float32) + rb1.astype(jnp.float32)
    gp = jax.nn.softmax(gl, axis=-1)
    g_idx = jnp.argmax(gl, axis=-1)
    g_w = jnp.take_along_axis(gp, g_idx[:, None], axis=-1)
    fl = jnp.einsum('td,dge->tge', t, rw2).astype(jnp.float32) + rb2.astype(jnp.float32)
    sel = jnp.broadcast_to(g_idx[:, None, None], (n_tok, 1, EXPERTS_PER_GROUP))
    fl_sel = jnp.take_along_axis(fl, sel, axis=1)[:, 0]
    top_v, top_i = lax.top_k(fl_sel, TOP_K_FINE)
    top_p = jax.nn.softmax(top_v, axis=-1) * g_w
    expert_id = g_idx[:, None] * EXPERTS_PER_GROUP + top_i
    combine = jnp.sum(jax.nn.one_hot(expert_id, N_EXPERTS, dtype=jnp.float32) * top_p[..., None], axis=1)
    y = jnp.zeros((n_tok, d), jnp.float32)
    for e in range(N_EXPERTS):
        he = jax.nn.silu(t @ w1[e]) * (t @ w3[e])
        y = y + combine[:, e:e + 1] * (he @ w2[e]).astype(jnp.float32)
    return y.astype(h.dtype).reshape(bsz, s, d)


def trunk(x, c, norm1_g, norm2_g, final_g, ada_w, ada_b, w_in, w_out, conv_w, conv_b,
          lru_wa, lru_ba, lru_wx, lru_bx, lru_lambda, dw_w, dw_b, cln_g, cln_b, lam_vec, subln_g,
          router_w1, router_b1, router_w2, router_b2, moe_w1, moe_w3, moe_w2):
    s = x.shape[1]
    pos = jnp.arange(s, dtype=jnp.int32)
    cs = jax.nn.silu(c)
    for l in range(DEPTH):
        mod = cs @ ada_w[l] + ada_b[l]
        sh1, sc1, g1, sh2, sc2, g2 = jnp.split(mod, 6, axis=-1)
        h = rmsnorm(x, norm1_g[l]) * (1.0 + sc1[:, None]) + sh1[:, None]
        x = x + g1[:, None] * mixer(h, pos, l, w_in[l], w_out[l], conv_w[l], conv_b[l],
                                    lru_wa[l], lru_ba[l], lru_wx[l], lru_bx[l], lru_lambda[l],
                                    dw_w[l], dw_b[l], cln_g[l], cln_b[l], lam_vec[l], subln_g[l])
        h = rmsnorm(x, norm2_g[l]) * (1.0 + sc2[:, None]) + sh2[:, None]
        x = x + g2[:, None] * hier_moe(h, router_w1[l], router_b1[l], router_w2[l], router_b2[l],
                                       moe_w1[l], moe_w3[l], moe_w2[l])
    return rmsnorm(x, final_g)


def setup_inputs(seed: int = 0) -> dict:
    key = jax.random.key(seed)
    ks = jax.random.split(key, 40)
    f32 = jnp.float32
    D = D_MODEL

    def nrm(k, shape, sc):
        return jax.random.normal(k, shape, f32) * sc

    u = jax.random.uniform(ks[17], (DEPTH, 2, LRU_WIDTH), f32, 0.9, 0.999)
    a_base = u ** (1.0 / LRU_C)
    gate_offset = jnp.repeat(jnp.array([0.0, 0.0, 1.0, 0.0, 0.0, 1.0], f32), D)
    return {
        'x_prompt': nrm(ks[0], (BATCH, SEQ, D), 1.0),
        'x_sample': nrm(ks[1], (DEC_BATCH, DEC_SEQ, D), 1.0),
        'c_prompt': nrm(ks[2], (BATCH, D), 1.0),
        'c_sample': nrm(ks[3], (DEC_BATCH, D), 1.0),
        'norm1_g': 1.0 + nrm(ks[4], (DEPTH, D), 0.02),
        'norm2_g': 1.0 + nrm(ks[5], (DEPTH, D), 0.02),
        'final_g': 1.0 + nrm(ks[6], (D,), 0.02),
        'ada_w': nrm(ks[7], (DEPTH, D, 6 * D), 0.2 * D ** -0.5),
        'ada_b': nrm(ks[8], (DEPTH, 6 * D), 0.02) + gate_offset,
        'w_in': nrm(ks[9], (DEPTH, D, IN_COLS), D ** -0.5),
        'w_out': nrm(ks[10], (DEPTH, D_MIX, D), D_MIX ** -0.5),
        'conv_w': nrm(ks[11], (DEPTH, LRU_CONV, LRU_WIDTH), LRU_CONV ** -0.5),
        'conv_b': nrm(ks[12], (DEPTH, LRU_WIDTH), 0.02),
        'lru_wa': nrm(ks[13], (DEPTH, 2, LRU_HEADS, LRU_HEAD_DIM, LRU_HEAD_DIM), LRU_HEAD_DIM ** -0.5),
        'lru_ba': nrm(ks[14], (DEPTH, 2, LRU_WIDTH), 0.02),
        'lru_wx': nrm(ks[15], (DEPTH, 2, LRU_HEADS, LRU_HEAD_DIM, LRU_HEAD_DIM), LRU_HEAD_DIM ** -0.5),
        'lru_bx': nrm(ks[16], (DEPTH, 2, LRU_WIDTH), 0.02),
        'lru_lambda': jnp.log(a_base) - jnp.log1p(-a_base),
        'dw_w': nrm(ks[18], (DEPTH, CONV_KERNEL, CONV_WIDTH), CONV_KERNEL ** -0.5),
        'dw_b': nrm(ks[19], (DEPTH, CONV_WIDTH), 0.02),
        'cln_g': 1.0 + nrm(ks[20], (DEPTH, CONV_WIDTH), 0.02),
        'cln_b': nrm(ks[21], (DEPTH, CONV_WIDTH), 0.02),
        'lam_vec': nrm(ks[22], (DEPTH, 4, ATT_QKDIM), 0.1),
        'subln_g': 1.0 + nrm(ks[23], (DEPTH, ATT_VDIM), 0.02),
        'router_w1': nrm(ks[24], (DEPTH, D, N_GROUPS), D ** -0.5),
        'router_b1': nrm(ks[25], (DEPTH, N_GROUPS), 0.01),
        'router_w2': nrm(ks[26], (DEPTH, D, N_GROUPS, EXPERTS_PER_GROUP), D ** -0.5),
        'router_b2': nrm(ks[27], (DEPTH, N_GROUPS, EXPERTS_PER_GROUP), 0.01),
        'moe_w1': nrm(ks[28], (DEPTH, N_EXPERTS, D, D_EXPERT), D ** -0.5),
        'moe_w3': nrm(ks[29], (DEPTH, N_EXPERTS, D, D_EXPERT), D ** -0.5),
        'moe_w2': nrm(ks[30], (DEPTH, N_EXPERTS, D_EXPERT, D), D_EXPERT ** -0.5),
    }


def reference(x_prompt, x_sample, c_prompt, c_sample, norm1_g, norm2_g, final_g, ada_w, ada_b,
              w_in, w_out, conv_w, conv_b, lru_wa, lru_ba, lru_wx, lru_bx, lru_lambda,
              dw_w, dw_b, cln_g, cln_b, lam_vec, subln_g,
              router_w1, router_b1, router_w2, router_b2, moe_w1, moe_w3, moe_w2):
    y_prompt = trunk(x_prompt, c_prompt, norm1_g, norm2_g, final_g, ada_w, ada_b, w_in, w_out,
                     conv_w, conv_b, lru_wa, lru_ba, lru_wx, lru_bx, lru_lambda, dw_w, dw_b,
                     cln_g, cln_b, lam_vec, subln_g, router_w1, router_b1, router_w2, router_b2,
                     moe_w1, moe_w3, moe_w2)
    y_sample = trunk(x_sample, c_sample, norm1_g, norm2_g, final_g, ada_w, ada_b, w_in, w_out,
                     conv_w, conv_b, lru_wa, lru_ba, lru_wx, lru_bx, lru_lambda, dw_w, dw_b,
                     cln_g, cln_b, lam_vec, subln_g, router_w1, router_b1, router_w2, router_b2,
                     moe_w1, moe_w3, moe_w2)
    return (y_prompt, y_sample)
```

```python
import functools
import math

import jax
import jax.numpy as jnp
from jax import lax
from jax.experimental import pallas as pl
from jax.experimental.pallas import tpu as pltpu

F32 = jnp.float32
BF16 = jnp.bfloat16

D_MODEL = 1024
DEPTH = 2
LRU_WIDTH = 256
LRU_HEADS = 4
LRU_CONV = 4
LRU_C = 8.0
CONV_WIDTH = 256
CONV_KERNEL = 31
ATT_WIDTH = 512
ATT_HEADS = 4
ATT_VDIM = 128
ATT_QKDIM = 64
ROPE_DIM = 16
ROPE_THETA = 500000.0
AB_COLS = 2 * LRU_WIDTH + 2 * CONV_WIDTH
QK_COLS = 2 * ATT_WIDTH
IN_COLS = AB_COLS + QK_COLS + ATT_WIDTH
N_GROUPS = 4
EXPERTS_PER_GROUP = 4
N_EXPERTS = 16
D_EXPERT = 512
EPS = 1e-6

LANES = 128
SUBLANES = 8
ROUTER_ROWS = 24
VMEM_LIMIT = 56 * 1024 * 1024

TOKEN_TILE = 512
SEQ_CHUNK = 512
Q_TILE = 256
LRU_HALO = 8
CONV_HALO = 16


def _cparams(*sem):
    return pltpu.CompilerParams(dimension_semantics=sem, vmem_limit_bytes=VMEM_LIMIT)


def _rms(x, g):
    return x * lax.rsqrt(jnp.mean(x * x, axis=-1, keepdims=True) + EPS) * g


def _ada_kernel(c_ref, w_ref, b_ref, o_ref):
    c = c_ref[...]
    cs = (c * jax.nn.sigmoid(c)).astype(BF16)
    o_ref[...] = jnp.dot(cs, w_ref[...].astype(BF16), preferred_element_type=F32) + b_ref[...]


def _ada_mod(c, ada_w, ada_b):
    nb = c.shape[0]
    tn = 1536
    out = pl.pallas_call(
        _ada_kernel,
        out_shape=jax.ShapeDtypeStruct((DEPTH, nb, 6 * D_MODEL), F32),
        grid=(DEPTH, 6 * D_MODEL // tn),
        in_specs=[
            pl.BlockSpec((nb, D_MODEL), lambda l, j: (0, 0)),
            pl.BlockSpec((None, D_MODEL, tn), lambda l, j: (l, 0, j)),
            pl.BlockSpec((None, 1, tn), lambda l, j: (l, 0, j)),
        ],
        out_specs=pl.BlockSpec((None, nb, tn), lambda l, j: (l, 0, j)),
        compiler_params=_cparams("arbitrary", "arbitrary"),
        name="ada_mod",
    )(c, ada_w, ada_b.reshape(DEPTH, 1, 6 * D_MODEL))
    return out.reshape(DEPTH, nb, 6, D_MODEL)


def _in_proj_kernel(x_ref, mod_ref, g_ref, w_ref, cos_ref, sa_ref, sb_ref, ab_ref, qk_ref, v_ref):
    x = x_ref[...]
    h = _rms(x, g_ref[...]) * (1.0 + mod_ref[1:2, :]) + mod_ref[0:1, :]
    hb = h.astype(BF16)
    ab_ref[...] = jnp.dot(hb, w_ref[:, 0:AB_COLS], preferred_element_type=F32)
    qk = jnp.dot(hb, w_ref[:, AB_COLS:AB_COLS + QK_COLS], preferred_element_type=F32)
    cos, sa, sb = cos_ref[...], sa_ref[...], sb_ref[...]
    for c in range(QK_COLS // LANES):
        blk = qk[:, c * LANES:(c + 1) * LANES]
        rot = (blk * cos + pltpu.roll(blk, ROPE_DIM // 2, 1) * sa
               + pltpu.roll(blk, LANES - ROPE_DIM // 2, 1) * sb)
        if c < ATT_WIDTH // LANES:
            rot = rot * (ATT_QKDIM ** -0.5)
        qk_ref[:, c * LANES:(c + 1) * LANES] = rot.astype(BF16)
    v_ref[...] = jnp.dot(hb, w_ref[:, AB_COLS + QK_COLS:], preferred_element_type=F32).astype(BF16)


def _rope_tables(s):
    half = ROPE_DIM // 2
    inv = ROPE_THETA ** (-(jnp.arange(half, dtype=F32) * 2.0) / ROPE_DIM)
    ang = jnp.arange(s, dtype=jnp.int32).astype(F32)[:, None] * inv[None, :]
    cos, sin = jnp.cos(ang), jnp.sin(ang)
    rest = ATT_QKDIM - ROPE_DIM
    one = jnp.ones((s, rest), F32)
    zero = jnp.zeros((s, rest), F32)
    zh = jnp.zeros((s, half), F32)
    reps = LANES // ATT_QKDIM
    cos_t = jnp.tile(jnp.concatenate([cos, cos, one], axis=1), (1, reps))
    sa_t = jnp.tile(jnp.concatenate([zh, sin, zero], axis=1), (1, reps))
    sb_t = jnp.tile(jnp.concatenate([-sin, zh, zero], axis=1), (1, reps))
    return cos_t, sa_t, sb_t


def _in_proj(x, mod, g, w_in_bf, tables, s):
    t = x.shape[0]
    tm = TOKEN_TILE
    per_seq = s // tm
    row = lambda i: (i, 0)
    pos = lambda i: (i % per_seq, 0)
    return pl.pallas_call(
        _in_proj_kernel,
        out_shape=(
            jax.ShapeDtypeStruct((t, AB_COLS), F32),
            jax.ShapeDtypeStruct((t, QK_COLS), BF16),
            jax.ShapeDtypeStruct((t, ATT_WIDTH), BF16),
        ),
        grid=(t // tm,),
        in_specs=[
            pl.BlockSpec((tm, D_MODEL), row),
            pl.BlockSpec((None, 6, D_MODEL), lambda i: (i // per_seq, 0, 0)),
            pl.BlockSpec((1, D_MODEL), lambda i: (0, 0)),
            pl.BlockSpec((D_MODEL, IN_COLS), lambda i: (0, 0)),
            pl.BlockSpec((tm, LANES), pos),
            pl.BlockSpec((tm, LANES), pos),
            pl.BlockSpec((tm, LANES), pos),
        ],
        out_specs=(
            pl.BlockSpec((tm, AB_COLS), row),
            pl.BlockSpec((tm, QK_COLS), row),
            pl.BlockSpec((tm, ATT_WIDTH), row),
        ),
        compiler_params=_cparams("arbitrary"),
        name="in_proj",
    )(x, mod, g, w_in_bf, *tables)


def _neg_expm1(y):
    series = -y * (1.0 + y * (1.0 / 2.0) * (1.0 + y * (1.0 / 3.0) * (1.0 + y * (1.0 / 4.0) * (1.0 + y * (1.0 / 5.0)))))
    return jnp.where(y > -1.0 / 64.0, series, 1.0 - jnp.exp(y))


def _tile_scan(a, b, reverse):
    n = a.shape[0]
    a3 = a.reshape(n // SUBLANES, SUBLANES, a.shape[1])
    b3 = b.reshape(n // SUBLANES, SUBLANES, b.shape[1])
    row = lax.broadcasted_iota(jnp.int32, a3.shape, 1)
    for d in (1, 2, 4):
        shift = SUBLANES - d if reverse else d
        a_s = pltpu.roll(a3, shift, 1)
        b_s = pltpu.roll(b3, shift, 1)
        m = (row < SUBLANES - d) if reverse else (row >= d)
        b3 = jnp.where(m, b3 + a3 * b_s, b3)
        a3 = jnp.where(m, a3 * a_s, a3)
    return a3.reshape(a.shape), b3.reshape(b.shape)


def _lru_kernel(cur_ref, prev_ref, next_ref, ga_ref, cw_ref, cb_ref, wa_ref, ba_ref, wx_ref, bx_ref,
                lam_ref, o_ref, hf_ref, ext_ref, a_ref, b_ref, h_ref, carry_ref, *, n_chunks):
    tc = SEQ_CHUNK
    p = pl.program_id(1)
    j = pl.program_id(2)
    jx = j + p * (n_chunks - 1 - 2 * j)

    @pl.when(j == 0)
    def _():
        carry_ref[...] = jnp.zeros_like(carry_ref)

    zero_halo = jnp.zeros((LRU_HALO, LRU_WIDTH), F32)
    ext_ref[0:LRU_HALO, :] = jnp.where(jx == 0, zero_halo, prev_ref[...])
    ext_ref[LRU_HALO:LRU_HALO + tc, :] = cur_ref[...]
    ext_ref[LRU_HALO + tc:, :] = jnp.where(jx == n_chunks - 1, zero_halo, next_ref[...])
    xc = cb_ref[...]
    for k in range(LRU_CONV):
        off = LRU_HALO - 2 + k
        xc = xc + cw_ref[k:k + 1, :] * ext_ref[off:off + tc, :]
    xb = xc.astype(BF16)
    r = jax.nn.sigmoid(jnp.dot(xb, wa_ref[...], preferred_element_type=F32) + ba_ref[...])
    gate_i = jax.nn.sigmoid(jnp.dot(xb, wx_ref[...], preferred_element_type=F32) + bx_ref[...])
    log_a = (-LRU_C) * r * jax.nn.softplus(-lam_ref[...])
    a = jnp.exp(log_a)
    b = jnp.sqrt(_neg_expm1(2.0 * log_a)) * (gate_i * xc)

    n_tiles = tc // SUBLANES

    def run(reverse):
        a_s, b_s = _tile_scan(a, b, reverse)
        a_ref[...] = a_s
        b_ref[...] = b_s

        def body(i, hc):
            ti = (n_tiles - 1 - i) if reverse else i
            off = pl.multiple_of(ti * SUBLANES, SUBLANES)
            h = b_ref[pl.ds(off, SUBLANES), :] + a_ref[pl.ds(off, SUBLANES), :] * hc
            h_ref[pl.ds(off, SUBLANES), :] = h
            edge = h[0:1, :] if reverse else h[SUBLANES - 1:SUBLANES, :]
            return jnp.broadcast_to(edge, (SUBLANES, LRU_WIDTH))

        carry_ref[...] = lax.fori_loop(0, n_tiles, body, carry_ref[...], unroll=8)

    row0 = pl.multiple_of(jx * tc, tc)

    @pl.when(p == 0)
    def _():
        run(False)
        hf_ref[pl.ds(row0, tc), :] = h_ref[...]

    @pl.when(p == 1)
    def _():
        run(True)
        hsum = hf_ref[pl.ds(row0, tc), :] + h_ref[...]
        o_ref[...] = (jax.nn.gelu(ga_ref[...]) * hsum).astype(BF16)


def _lru(ab3, conv_w, conv_b, wa_bd, ba, wx_bd, bx, lam):
    bsz, s, _ = ab3.shape
    tc = SEQ_CHUNK
    n = s // tc
    hb = tc // LRU_HALO
    jx = lambda p, j: j + p * (n - 1 - 2 * j)
    jg = lambda p, j: n - 1 - p * j
    w2 = lambda b, p, j: (0, 0)
    wdir = lambda b, p, j: (p, 0, 0)
    return pl.pallas_call(
        functools.partial(_lru_kernel, n_chunks=n),
        out_shape=jax.ShapeDtypeStruct((bsz, s, LRU_WIDTH), BF16),
        grid=(bsz, 2, n),
        in_specs=[
            pl.BlockSpec((None, tc, LRU_WIDTH), lambda b, p, j: (b, jx(p, j), 0)),
            pl.BlockSpec((None, LRU_HALO, LRU_WIDTH),
                         lambda b, p, j: (b, jnp.maximum(jx(p, j) * hb - 1, 0), 0)),
            pl.BlockSpec((None, LRU_HALO, LRU_WIDTH),
                         lambda b, p, j: (b, jnp.minimum((jx(p, j) + 1) * hb, s // LRU_HALO - 1), 0)),
            pl.BlockSpec((None, tc, LRU_WIDTH), lambda b, p, j: (b, jg(p, j), 1)),
            pl.BlockSpec((LRU_CONV, LRU_WIDTH), w2),
            pl.BlockSpec((1, LRU_WIDTH), w2),
            pl.BlockSpec((None, LRU_WIDTH, LRU_WIDTH), wdir),
            pl.BlockSpec((None, 1, LRU_WIDTH), wdir),
            pl.BlockSpec((None, LRU_WIDTH, LRU_WIDTH), wdir),
            pl.BlockSpec((None, 1, LRU_WIDTH), wdir),
            pl.BlockSpec((None, 1, LRU_WIDTH), wdir),
        ],
        out_specs=pl.BlockSpec((None, tc, LRU_WIDTH), lambda b, p, j: (b, jg(p, j), 0)),
        scratch_shapes=[
            pltpu.VMEM((s, LRU_WIDTH), F32),
            pltpu.VMEM((tc + 2 * LRU_HALO, LRU_WIDTH), F32),
            pltpu.VMEM((tc, LRU_WIDTH), F32),
            pltpu.VMEM((tc, LRU_WIDTH), F32),
            pltpu.VMEM((tc, LRU_WIDTH), F32),
            pltpu.VMEM((SUBLANES, LRU_WIDTH), F32),
        ],
        compiler_params=_cparams("arbitrary", "arbitrary", "arbitrary"),
        name="lru",
    )(ab3, ab3, ab3, ab3, conv_w, conv_b, wa_bd, ba, wx_bd, bx, lam)


def _convmod_kernel(u_ref, g_ref, up_ref, gp_ref, un_ref, gn_ref, w_ref, b_ref, lg_ref, lb_ref,
                    o_ref, ext_ref, *, n_chunks):
    tc = SEQ_CHUNK
    j = pl.program_id(1)
    zero_halo = jnp.zeros((CONV_HALO, CONV_WIDTH), F32)
    glu = lambda u, g: u * jax.nn.sigmoid(g)
    ext_ref[0:CONV_HALO, :] = jnp.where(j == 0, zero_halo, glu(up_ref[...], gp_ref[...]))
    ext_ref[CONV_HALO:CONV_HALO + tc, :] = glu(u_ref[...], g_ref[...])
    ext_ref[CONV_HALO + tc:, :] = jnp.where(j == n_chunks - 1, zero_halo, glu(un_ref[...], gn_ref[...]))
    acc = b_ref[...]
    for k in range(CONV_KERNEL):
        off = CONV_HALO - CONV_KERNEL // 2 + k
        acc = acc + w_ref[k:k + 1, :] * ext_ref[off:off + tc, :]
    mu = jnp.mean(acc, axis=-1, keepdims=True)
    cen = acc - mu
    var = jnp.mean(cen * cen, axis=-1, keepdims=True)
    z = cen * lax.rsqrt(var + EPS) * lg_ref[...] + lb_ref[...]
    o_ref[...] = (z * jax.nn.sigmoid(z)).astype(BF16)


def _convmod(ab3, dw_w, dw_b, cln_g, cln_b):
    bsz, s, _ = ab3.shape
    tc = SEQ_CHUNK
    n = s // tc
    hb = tc // CONV_HALO
    w2 = lambda b, j: (0, 0)
    ucol = 2 * LRU_WIDTH // CONV_WIDTH
    gcol = ucol + 1
    cur = lambda col: pl.BlockSpec((None, tc, CONV_WIDTH), lambda b, j: (b, j, col))
    prev = lambda col: pl.BlockSpec((None, CONV_HALO, CONV_WIDTH),
                                    lambda b, j: (b, jnp.maximum(j * hb - 1, 0), col))
    nxt = lambda col: pl.BlockSpec((None, CONV_HALO, CONV_WIDTH),
                                   lambda b, j: (b, jnp.minimum((j + 1) * hb, s // CONV_HALO - 1), col))
    return pl.pallas_call(
        functools.partial(_convmod_kernel, n_chunks=n),
        out_shape=jax.ShapeDtypeStruct((bsz, s, CONV_WIDTH), BF16),
        grid=(bsz, n),
        in_specs=[
            cur(ucol), cur(gcol), prev(ucol), prev(gcol), nxt(ucol), nxt(gcol),
            pl.BlockSpec((CONV_KERNEL, CONV_WIDTH), w2),
            pl.BlockSpec((1, CONV_WIDTH), w2),
            pl.BlockSpec((1, CONV_WIDTH), w2),
            pl.BlockSpec((1, CONV_WIDTH), w2),
        ],
        out_specs=pl.BlockSpec((None, tc, CONV_WIDTH), lambda b, j: (b, j, 0)),
        scratch_shapes=[pltpu.VMEM((tc + 2 * CONV_HALO, CONV_WIDTH), F32)],
        compiler_params=_cparams("arbitrary", "arbitrary"),
        name="convmod",
    )(ab3, ab3, ab3, ab3, ab3, ab3, dw_w, dw_b, cln_g, cln_b)


def _attn_kernel(q_ref, k_ref, v_ref, lv_ref, sg_ref, o_ref, *, lam_init):
    tq = Q_TILE
    q = q_ref[...]
    lane = lax.broadcasted_iota(jnp.int32, q.shape, 1)
    zero = jnp.zeros_like(q)
    qq = jnp.concatenate([jnp.where(lane < ATT_QKDIM, q, zero), jnp.where(lane >= ATT_QKDIM, q, zero)], axis=0)
    s = lax.dot_general(qq, k_ref[...], (((1,), (1,)), ((), ())), preferred_element_type=F32)
    e = jnp.exp(s - jnp.max(s, axis=-1, keepdims=True))
    rinv = 1.0 / jnp.sum(e, axis=-1, keepdims=True)
    lv = lv_ref[...]
    lam = (jnp.exp(jnp.sum(lv[0:1, :] * lv[1:2, :], axis=-1, keepdims=True))
           - jnp.exp(jnp.sum(lv[2:3, :] * lv[3:4, :], axis=-1, keepdims=True)) + lam_init)
    w = e[0:tq, :] * rinv[0:tq, :] - lam * (e[tq:, :] * rinv[tq:, :])
    o = jnp.dot(w.astype(BF16), v_ref[...], preferred_element_type=F32)
    o_ref[...] = (_rms(o, sg_ref[...]) * (1.0 - lam_init)).astype(BF16)


def _attn(qk3, v3, lam_vec, subln_g, lam_init):
    bsz, s, _ = qk3.shape
    tq = Q_TILE
    kcol = ATT_WIDTH // LANES
    w2 = lambda b, h, i: (0, 0)
    return pl.pallas_call(
        functools.partial(_attn_kernel, lam_init=lam_init),
        out_shape=jax.ShapeDtypeStruct((bsz, s, ATT_WIDTH), BF16),
        grid=(bsz, ATT_HEADS, s // tq),
        in_specs=[
            pl.BlockSpec((None, tq, LANES), lambda b, h, i: (b, i, h)),
            pl.BlockSpec((None, s, LANES), lambda b, h, i: (b, 0, kcol + h)),
            pl.BlockSpec((None, s, ATT_VDIM), lambda b, h, i: (b, 0, h)),
            pl.BlockSpec((4, ATT_QKDIM), w2),
            pl.BlockSpec((1, ATT_VDIM), w2),
        ],
        out_specs=pl.BlockSpec((None, tq, ATT_VDIM), lambda b, h, i: (b, i, h)),
        compiler_params=_cparams("arbitrary", "arbitrary", "arbitrary"),
        name="attn",
    )(qk3, qk3, v3, lam_vec, subln_g)


def _first_argmax(vals):
    best = vals[0]
    idx = jnp.zeros(best.shape, jnp.int32)
    for i in range(1, len(vals)):
        better = vals[i] > best
        idx = jnp.where(better, i, idx)
        best = jnp.where(better, vals[i], best)
    return idx, best


def _out_proj_kernel(x_ref, ya_ref, yb_ref, yc_ref, w_ref, mod_ref, g_ref, rw_ref, rb_ref,
                     xo_ref, h_ref, comb_ref):
    y = jnp.dot(ya_ref[...], w_ref[0:LRU_WIDTH, :], preferred_element_type=F32)
    y = y + jnp.dot(yb_ref[...], w_ref[LRU_WIDTH:LRU_WIDTH + CONV_WIDTH, :], preferred_element_type=F32)
    y = y + jnp.dot(yc_ref[...], w_ref[LRU_WIDTH + CONV_WIDTH:, :], preferred_element_type=F32)
    x = x_ref[...] + mod_ref[2:3, :] * y
    xo_ref[...] = x
    h = _rms(x, g_ref[...]) * (1.0 + mod_ref[4:5, :]) + mod_ref[3:4, :]
    h_ref[...] = h.astype(BF16)

    logits = lax.dot_general(rw_ref[...], h, (((1,), (1,)), ((), ())), precision=lax.Precision.HIGHEST,
                             preferred_element_type=F32) + rb_ref[...]
    gl = [logits[g:g + 1, :] for g in range(N_GROUPS)]
    g_idx, g_max = _first_argmax(gl)
    denom = gl[0] * 0.0
    for g in range(N_GROUPS):
        denom = denom + jnp.exp(gl[g] - g_max)
    g_w = 1.0 / denom
    fl = []
    for e in range(EXPERTS_PER_GROUP):
        sel = logits[N_GROUPS + e:N_GROUPS + e + 1, :]
        for g in range(1, N_GROUPS):
            r0 = N_GROUPS + g * EXPERTS_PER_GROUP + e
            sel = jnp.where(g_idx == g, logits[r0:r0 + 1, :], sel)
        fl.append(sel)
    i1, v1 = _first_argmax(fl)
    neg = jnp.full(v1.shape, -jnp.inf, F32)
    i2, v2 = _first_argmax([jnp.where(i1 == e, neg, fl[e]) for e in range(EXPERTS_PER_GROUP)])
    t = jnp.exp(v2 - v1)
    p1 = g_w / (1.0 + t)
    p2 = g_w * t / (1.0 + t)
    e1 = g_idx * EXPERTS_PER_GROUP + i1
    e2 = g_idx * EXPERTS_PER_GROUP + i2
    zero = jnp.zeros_like(p1)
    rows = [jnp.where(e1 == e, p1, zero) + jnp.where(e2 == e, p2, zero) for e in range(N_EXPERTS)]
    rows.append(jnp.zeros((LANES - N_EXPERTS, p1.shape[1]), F32))
    comb_ref[...] = jnp.concatenate(rows, axis=0).T


def _out_proj(x, ya, yb, yc, w_out_bf, mod, g2, rw_t, rb, s):
    t = x.shape[0]
    tm = TOKEN_TILE
    per_seq = s // tm
    row = lambda i: (i, 0)
    w2 = lambda i: (0, 0)
    return pl.pallas_call(
        _out_proj_kernel,
        out_shape=(
            jax.ShapeDtypeStruct((t, D_MODEL), F32),
            jax.ShapeDtypeStruct((t, D_MODEL), BF16),
            jax.ShapeDtypeStruct((t, LANES), F32),
        ),
        grid=(t // tm,),
        in_specs=[
            pl.BlockSpec((tm, D_MODEL), row),
            pl.BlockSpec((tm, LRU_WIDTH), row),
            pl.BlockSpec((tm, CONV_WIDTH), row),
            pl.BlockSpec((tm, ATT_WIDTH), row),
            pl.BlockSpec((D_MODEL, D_MODEL), w2),
            pl.BlockSpec((None, 6, D_MODEL), lambda i: (i // per_seq, 0, 0)),
            pl.BlockSpec((1, D_MODEL), w2),
            pl.BlockSpec((ROUTER_ROWS, D_MODEL), w2),
            pl.BlockSpec((ROUTER_ROWS, 1), w2),
        ],
        out_specs=(
            pl.BlockSpec((tm, D_MODEL), row),
            pl.BlockSpec((tm, D_MODEL), row),
            pl.BlockSpec((tm, LANES), row),
        ),
        compiler_params=_cparams("arbitrary"),
        name="out_proj",
    )(x, ya, yb, yc, w_out_bf, mod, g2, rw_t, rb)


def _moe_kernel(eid_ref, h_ref, w13_ref, w2_ref, comb_ref, o_ref):
    del eid_ref
    slot = pl.program_id(1)
    h13 = jnp.dot(h_ref[...], w13_ref[...], preferred_element_type=F32)
    a = h13[:, 0:D_EXPERT]
    he = (a * jax.nn.sigmoid(a)) * h13[:, D_EXPERT:]
    ye = jnp.dot(he.astype(BF16), w2_ref[...], preferred_element_type=F32)
    comb = comb_ref[...]
    lane = lax.broadcasted_iota(jnp.int32, comb.shape, 1)
    c = jnp.sum(jnp.where(lane == slot, comb, 0.0), axis=-1, keepdims=True)

    @pl.when(slot == 0)
    def _():
        o_ref[...] = c * ye

    @pl.when(slot > 0)
    def _():
        o_ref[...] += c * ye


def _moe(expert_ids, h, w13, w2, comb, n_slots):
    t = h.shape[0]
    tm = TOKEN_TILE
    return pl.pallas_call(
        _moe_kernel,
        out_shape=jax.ShapeDtypeStruct((t, D_MODEL), F32),
        grid_spec=pltpu.PrefetchScalarGridSpec(
            num_scalar_prefetch=1,
            grid=(t // tm, n_slots),
            in_specs=[
                pl.BlockSpec((tm, D_MODEL), lambda i, s, eid: (i, 0)),
                pl.BlockSpec((None, D_MODEL, 2 * D_EXPERT), lambda i, s, eid: (eid[i * n_slots + s], 0, 0)),
                pl.BlockSpec((None, D_EXPERT, D_MODEL), lambda i, s, eid: (eid[i * n_slots + s], 0, 0)),
                pl.BlockSpec((tm, LANES), lambda i, s, eid: (i, 0)),
            ],
            out_specs=pl.BlockSpec((tm, D_MODEL), lambda i, s, eid: (i, 0)),
        ),
        compiler_params=_cparams("arbitrary", "arbitrary"),
        name="moe",
    )(expert_ids, h, w13, w2, comb)


def _resid_kernel(x_ref, y_ref, mod_ref, g_ref, o_ref, *, final):
    x = x_ref[...] + mod_ref[5:6, :] * y_ref[...]
    o_ref[...] = _rms(x, g_ref[...]) if final else x


def _resid(x, y, mod, final_g, s, final):
    t = x.shape[0]
    tm = TOKEN_TILE
    per_seq = s // tm
    row = lambda i: (i, 0)
    return pl.pallas_call(
        functools.partial(_resid_kernel, final=final),
        out_shape=jax.ShapeDtypeStruct((t, D_MODEL), F32),
        grid=(t // tm,),
        in_specs=[
            pl.BlockSpec((tm, D_MODEL), row),
            pl.BlockSpec((tm, D_MODEL), row),
            pl.BlockSpec((None, 6, D_MODEL), lambda i: (i // per_seq, 0, 0)),
            pl.BlockSpec((1, D_MODEL), lambda i: (0, 0)),
        ],
        out_specs=pl.BlockSpec((tm, D_MODEL), row),
        compiler_params=_cparams("arbitrary"),
        name="resid",
    )(x, y, mod, final_g)


def _block_diag(w):
    h, dh, _ = w.shape
    eye = jnp.eye(h, dtype=w.dtype)
    return (eye[:, None, :, None] * w[:, :, None, :]).reshape(h * dh, h * dh)


def _prep_layer(l, p):
    row = lambda a: a.reshape(1, -1)
    rw = jnp.concatenate([p["router_w1"][l], p["router_w2"][l].reshape(D_MODEL, N_EXPERTS)], axis=1)
    rw = jnp.pad(rw, ((0, 0), (0, ROUTER_ROWS - rw.shape[1])))
    rb = jnp.concatenate([p["router_b1"][l], p["router_b2"][l].reshape(N_EXPERTS)])
    rb = jnp.pad(rb, (0, ROUTER_ROWS - rb.shape[0]))
    return dict(
        norm1_g=row(p["norm1_g"][l]),
        norm2_g=row(p["norm2_g"][l]),
        w_in=p["w_in"][l].astype(BF16),
        w_out=p["w_out"][l].astype(BF16),
        conv_w=p["conv_w"][l],
        conv_b=row(p["conv_b"][l]),
        wa_bd=jnp.stack([_block_diag(p["lru_wa"][l, d]) for d in range(2)]).astype(BF16),
        wx_bd=jnp.stack([_block_diag(p["lru_wx"][l, d]) for d in range(2)]).astype(BF16),
        ba=p["lru_ba"][l].reshape(2, 1, LRU_WIDTH),
        bx=p["lru_bx"][l].reshape(2, 1, LRU_WIDTH),
        lam=p["lru_lambda"][l].reshape(2, 1, LRU_WIDTH),
        dw_w=p["dw_w"][l],
        dw_b=row(p["dw_b"][l]),
        cln_g=row(p["cln_g"][l]),
        cln_b=row(p["cln_b"][l]),
        lam_vec=p["lam_vec"][l],
        subln_g=row(p["subln_g"][l]),
        rw_t=rw.T,
        rb=rb.reshape(ROUTER_ROWS, 1),
        w13=jnp.concatenate([p["moe_w1"][l], p["moe_w3"][l]], axis=-1).astype(BF16),
        w2=p["moe_w2"][l].astype(BF16),
    )


def _trunk(x, mods, layers, final_g):
    bsz, s, _ = x.shape
    t = bsz * s
    tables = _rope_tables(s)
    x = x.reshape(t, D_MODEL)
    dense_ids = jnp.tile(jnp.arange(N_EXPERTS, dtype=jnp.int32), t // TOKEN_TILE)
    for l, w in enumerate(layers):
        mod = mods[l]
        ab, qk, v = _in_proj(x, mod, w["norm1_g"], w["w_in"], tables, s)
        ab3 = ab.reshape(bsz, s, AB_COLS)
        ya = _lru(ab3, w["conv_w"], w["conv_b"], w["wa_bd"], w["ba"], w["wx_bd"], w["bx"], w["lam"])
        yb = _convmod(ab3, w["dw_w"], w["dw_b"], w["cln_g"], w["cln_b"])
        lam_init = 0.8 - 0.6 * math.exp(-0.3 * l)
        yc = _attn(qk.reshape(bsz, s, QK_COLS), v.reshape(bsz, s, ATT_WIDTH), w["lam_vec"], w["subln_g"], lam_init)
        x, h2, comb = _out_proj(x, ya.reshape(t, LRU_WIDTH), yb.reshape(t, CONV_WIDTH), yc.reshape(t, ATT_WIDTH),
                                w["w_out"], mod, w["norm2_g"], w["rw_t"], w["rb"], s)
        y = _moe(dense_ids, h2, w["w13"], w["w2"], comb, N_EXPERTS)
        x = _resid(x, y, mod, final_g, s, final=(l == DEPTH - 1))
    return x.reshape(bsz, s, D_MODEL)


def kernel(x_prompt, x_sample, c_prompt, c_sample, norm1_g, norm2_g, final_g, ada_w, ada_b, w_in, w_out, conv_w, conv_b, lru_wa, lru_ba, lru_wx, lru_bx, lru_lambda, dw_w, dw_b, cln_g, cln_b, lam_vec, subln_g, router_w1, router_b1, router_w2, router_b2, moe_w1, moe_w3, moe_w2):
    p = dict(norm1_g=norm1_g, norm2_g=norm2_g, w_in=w_in, w_out=w_out, conv_w=conv_w, conv_b=conv_b,
             lru_wa=lru_wa, lru_ba=lru_ba, lru_wx=lru_wx, lru_bx=lru_bx, lru_lambda=lru_lambda,
             dw_w=dw_w, dw_b=dw_b, cln_g=cln_g, cln_b=cln_b, lam_vec=lam_vec, subln_g=subln_g,
             router_w1=router_w1, router_b1=router_b1, router_w2=router_w2, router_b2=router_b2,
             moe_w1=moe_w1, moe_w3=moe_w3, moe_w2=moe_w2)
    layers = [_prep_layer(l, p) for l in range(DEPTH)]
    nb = c_prompt.shape[0]
    mods = _ada_mod(jnp.concatenate([c_prompt, c_sample], axis=0), ada_w, ada_b)
    fg = final_g.reshape(1, D_MODEL)
    y_prompt = _trunk(x_prompt, mods[:, :nb], layers, fg)
    y_sample = _trunk(x_sample, mods[:, nb:], layers, fg)
    return (y_prompt, y_sample)
```

```python
import functools
import math

import jax
import jax.numpy as jnp
from jax import lax
from jax.experimental import pallas as pl
from jax.experimental.pallas import tpu as pltpu

F32 = jnp.float32
BF16 = jnp.bfloat16

D_MODEL = 1024
DEPTH = 2
LRU_WIDTH = 256
LRU_HEADS = 4
LRU_CONV = 4
LRU_C = 8.0
CONV_WIDTH = 256
CONV_KERNEL = 31
ATT_WIDTH = 512
ATT_HEADS = 4
ATT_VDIM = 128
ATT_QKDIM = 64
ROPE_DIM = 16
ROPE_THETA = 500000.0
AB_COLS = 2 * LRU_WIDTH + 2 * CONV_WIDTH
QK_COLS = 2 * ATT_WIDTH
IN_COLS = AB_COLS + QK_COLS + ATT_WIDTH
N_GROUPS = 4
EXPERTS_PER_GROUP = 4
N_EXPERTS = 16
D_EXPERT = 512
EPS = 1e-6

LANES = 128
SUBLANES = 8
ROUTER_ROWS = 24
PAIRS_PER_GROUP = 6
N_BUCKETS = N_GROUPS * PAIRS_PER_GROUP
BUCKET_ROWS = 32
PAYLOAD_COLS = D_MODEL + LANES
VMEM_LIMIT = 56 * 1024 * 1024

TOKEN_TILE = 512
SEQ_CHUNK = 512
Q_TILE = 256
LRU_HALO = 8
CONV_HALO = 16


def _cparams(*sem):
    return pltpu.CompilerParams(dimension_semantics=sem, vmem_limit_bytes=VMEM_LIMIT)


def _rms(x, g):
    return x * lax.rsqrt(jnp.mean(x * x, axis=-1, keepdims=True) + EPS) * g


def _ada_kernel(c_ref, w_ref, b_ref, o_ref):
    c = c_ref[...]
    cs = (c * jax.nn.sigmoid(c)).astype(BF16)
    o_ref[...] = jnp.dot(cs, w_ref[...].astype(BF16), preferred_element_type=F32) + b_ref[...]


def _ada_mod(c, ada_w, ada_b):
    nb = c.shape[0]
    tn = 1536
    out = pl.pallas_call(
        _ada_kernel,
        out_shape=jax.ShapeDtypeStruct((DEPTH, nb, 6 * D_MODEL), F32),
        grid=(DEPTH, 6 * D_MODEL // tn),
        in_specs=[
            pl.BlockSpec((nb, D_MODEL), lambda l, j: (0, 0)),
            pl.BlockSpec((None, D_MODEL, tn), lambda l, j: (l, 0, j)),
            pl.BlockSpec((None, 1, tn), lambda l, j: (l, 0, j)),
        ],
        out_specs=pl.BlockSpec((None, nb, tn), lambda l, j: (l, 0, j)),
        compiler_params=_cparams("arbitrary", "arbitrary"),
        name="ada_mod",
    )(c, ada_w, ada_b.reshape(DEPTH, 1, 6 * D_MODEL))
    return out.reshape(DEPTH, nb, 6, D_MODEL)


def _in_proj_kernel(x_ref, mod_ref, g_ref, w_ref, cos_ref, sa_ref, sb_ref, ab_ref, qk_ref, vt_ref):
    x = x_ref[...]
    h = _rms(x, g_ref[...]) * (1.0 + mod_ref[1:2, :]) + mod_ref[0:1, :]
    hb = h.astype(BF16)
    ab_ref[...] = jnp.dot(hb, w_ref[:, 0:AB_COLS], preferred_element_type=F32)
    qk = jnp.dot(hb, w_ref[:, AB_COLS:AB_COLS + QK_COLS], preferred_element_type=F32)
    cos, sa, sb = cos_ref[...], sa_ref[...], sb_ref[...]
    for c in range(QK_COLS // LANES):
        blk = qk[:, c * LANES:(c + 1) * LANES]
        rot = (blk * cos + pltpu.roll(blk, ROPE_DIM // 2, 1) * sa
               + pltpu.roll(blk, LANES - ROPE_DIM // 2, 1) * sb)
        if c < ATT_WIDTH // LANES:
            rot = rot * (ATT_QKDIM ** -0.5)
        qk_ref[:, c * LANES:(c + 1) * LANES] = rot.astype(BF16)
    v = jnp.dot(hb, w_ref[:, AB_COLS + QK_COLS:], preferred_element_type=F32)
    vt_ref[...] = v.T.astype(BF16)


def _rope_tables(s):
    half = ROPE_DIM // 2
    inv = ROPE_THETA ** (-(jnp.arange(half, dtype=F32) * 2.0) / ROPE_DIM)
    ang = jnp.arange(s, dtype=jnp.int32).astype(F32)[:, None] * inv[None, :]
    cos, sin = jnp.cos(ang), jnp.sin(ang)
    rest = ATT_QKDIM - ROPE_DIM
    one = jnp.ones((s, rest), F32)
    zero = jnp.zeros((s, rest), F32)
    zh = jnp.zeros((s, half), F32)
    reps = LANES // ATT_QKDIM
    cos_t = jnp.tile(jnp.concatenate([cos, cos, one], axis=1), (1, reps))
    sa_t = jnp.tile(jnp.concatenate([zh, sin, zero], axis=1), (1, reps))
    sb_t = jnp.tile(jnp.concatenate([-sin, zh, zero], axis=1), (1, reps))
    return cos_t, sa_t, sb_t


def _in_proj(x, mod, g, w_in_bf, tables, s):
    t = x.shape[0]
    tm = TOKEN_TILE
    per_seq = s // tm
    row = lambda i: (i, 0)
    pos = lambda i: (i % per_seq, 0)
    return pl.pallas_call(
        _in_proj_kernel,
        out_shape=(
            jax.ShapeDtypeStruct((t, AB_COLS), F32),
            jax.ShapeDtypeStruct((t, QK_COLS), BF16),
            jax.ShapeDtypeStruct((t // s, ATT_WIDTH, s), BF16),
        ),
        grid=(t // tm,),
        in_specs=[
            pl.BlockSpec((tm, D_MODEL), row),
            pl.BlockSpec((None, 6, D_MODEL), lambda i: (i // per_seq, 0, 0)),
            pl.BlockSpec((1, D_MODEL), lambda i: (0, 0)),
            pl.BlockSpec((D_MODEL, IN_COLS), lambda i: (0, 0)),
            pl.BlockSpec((tm, LANES), pos),
            pl.BlockSpec((tm, LANES), pos),
            pl.BlockSpec((tm, LANES), pos),
        ],
        out_specs=(
            pl.BlockSpec((tm, AB_COLS), row),
            pl.BlockSpec((tm, QK_COLS), row),
            pl.BlockSpec((None, ATT_WIDTH, tm), lambda i: (i // per_seq, 0, i % per_seq)),
        ),
        compiler_params=_cparams("arbitrary"),
        name="in_proj",
    )(x, mod, g, w_in_bf, *tables)


def _neg_expm1(y):
    series = -y * (1.0 + y * (1.0 / 2.0) * (1.0 + y * (1.0 / 3.0) * (1.0 + y * (1.0 / 4.0) * (1.0 + y * (1.0 / 5.0)))))
    return jnp.where(y > -1.0 / 64.0, series, 1.0 - jnp.exp(y))


def _tile_scan(a, b, reverse):
    n = a.shape[0]
    a3 = a.reshape(n // SUBLANES, SUBLANES, a.shape[1])
    b3 = b.reshape(n // SUBLANES, SUBLANES, b.shape[1])
    row = lax.broadcasted_iota(jnp.int32, a3.shape, 1)
    for d in (1, 2, 4):
        shift = SUBLANES - d if reverse else d
        a_s = pltpu.roll(a3, shift, 1)
        b_s = pltpu.roll(b3, shift, 1)
        m = (row < SUBLANES - d) if reverse else (row >= d)
        b3 = jnp.where(m, b3 + a3 * b_s, b3)
        a3 = jnp.where(m, a3 * a_s, a3)
    return a3.reshape(a.shape), b3.reshape(b.shape)


def _lru_kernel(cur_ref, prev_ref, next_ref, ga_ref, cw_ref, cb_ref, wa_ref, ba_ref, wx_ref, bx_ref,
                lam_ref, o_ref, hf_ref, ext_ref, a_ref, b_ref, h_ref, carry_ref, *, n_chunks):
    tc = SEQ_CHUNK
    p = pl.program_id(1)
    j = pl.program_id(2)
    jx = j + p * (n_chunks - 1 - 2 * j)

    @pl.when(j == 0)
    def _():
        carry_ref[...] = jnp.zeros_like(carry_ref)

    zero_halo = jnp.zeros((LRU_HALO, LRU_WIDTH), F32)
    ext_ref[0:LRU_HALO, :] = jnp.where(jx == 0, zero_halo, prev_ref[...])
    ext_ref[LRU_HALO:LRU_HALO + tc, :] = cur_ref[...]
    ext_ref[LRU_HALO + tc:, :] = jnp.where(jx == n_chunks - 1, zero_halo, next_ref[...])
    xc = cb_ref[...]
    for k in range(LRU_CONV):
        off = LRU_HALO - 2 + k
        xc = xc + cw_ref[k:k + 1, :] * ext_ref[off:off + tc, :]
    xb = xc.astype(BF16)
    r = jax.nn.sigmoid(jnp.dot(xb, wa_ref[...], preferred_element_type=F32) + ba_ref[...])
    gate_i = jax.nn.sigmoid(jnp.dot(xb, wx_ref[...], preferred_element_type=F32) + bx_ref[...])
    log_a = (-LRU_C) * r * jax.nn.softplus(-lam_ref[...])
    a = jnp.exp(log_a)
    b = jnp.sqrt(_neg_expm1(2.0 * log_a)) * (gate_i * xc)

    n_tiles = tc // SUBLANES

    def run(reverse):
        a_s, b_s = _tile_scan(a, b, reverse)
        a_ref[...] = a_s
        b_ref[...] = b_s

        def body(i, hc):
            ti = (n_tiles - 1 - i) if reverse else i
            off = pl.multiple_of(ti * SUBLANES, SUBLANES)
            h = b_ref[pl.ds(off, SUBLANES), :] + a_ref[pl.ds(off, SUBLANES), :] * hc
            h_ref[pl.ds(off, SUBLANES), :] = h
            edge = h[0:1, :] if reverse else h[SUBLANES - 1:SUBLANES, :]
            return jnp.broadcast_to(edge, (SUBLANES, LRU_WIDTH))

        carry_ref[...] = lax.fori_loop(0, n_tiles, body, carry_ref[...], unroll=8)

    row0 = pl.multiple_of(jx * tc, tc)

    @pl.when(p == 0)
    def _():
        run(False)
        hf_ref[pl.ds(row0, tc), :] = h_ref[...]

    @pl.when(p == 1)
    def _():
        run(True)
        hsum = hf_ref[pl.ds(row0, tc), :] + h_ref[...]
        o_ref[...] = (jax.nn.gelu(ga_ref[...]) * hsum).astype(BF16)


def _lru(ab3, conv_w, conv_b, wa_bd, ba, wx_bd, bx, lam):
    bsz, s, _ = ab3.shape
    tc = SEQ_CHUNK
    n = s // tc
    hb = tc // LRU_HALO
    jx = lambda p, j: j + p * (n - 1 - 2 * j)
    jg = lambda p, j: n - 1 - p * j
    w2 = lambda b, p, j: (0, 0)
    wdir = lambda b, p, j: (p, 0, 0)
    return pl.pallas_call(
        functools.partial(_lru_kernel, n_chunks=n),
        out_shape=jax.ShapeDtypeStruct((bsz, s, LRU_WIDTH), BF16),
        grid=(bsz, 2, n),
        in_specs=[
            pl.BlockSpec((None, tc, LRU_WIDTH), lambda b, p, j: (b, jx(p, j), 0)),
            pl.BlockSpec((None, LRU_HALO, LRU_WIDTH),
                         lambda b, p, j: (b, jnp.maximum(jx(p, j) * hb - 1, 0), 0)),
            pl.BlockSpec((None, LRU_HALO, LRU_WIDTH),
                         lambda b, p, j: (b, jnp.minimum((jx(p, j) + 1) * hb, s // LRU_HALO - 1), 0)),
            pl.BlockSpec((None, tc, LRU_WIDTH), lambda b, p, j: (b, jg(p, j), 1)),
            pl.BlockSpec((LRU_CONV, LRU_WIDTH), w2),
            pl.BlockSpec((1, LRU_WIDTH), w2),
            pl.BlockSpec((None, LRU_WIDTH, LRU_WIDTH), wdir),
            pl.BlockSpec((None, 1, LRU_WIDTH), wdir),
            pl.BlockSpec((None, LRU_WIDTH, LRU_WIDTH), wdir),
            pl.BlockSpec((None, 1, LRU_WIDTH), wdir),
            pl.BlockSpec((None, 1, LRU_WIDTH), wdir),
        ],
        out_specs=pl.BlockSpec((None, tc, LRU_WIDTH), lambda b, p, j: (b, jg(p, j), 0)),
        scratch_shapes=[
            pltpu.VMEM((s, LRU_WIDTH), F32),
            pltpu.VMEM((tc + 2 * LRU_HALO, LRU_WIDTH), F32),
            pltpu.VMEM((tc, LRU_WIDTH), F32),
            pltpu.VMEM((tc, LRU_WIDTH), F32),
            pltpu.VMEM((tc, LRU_WIDTH), F32),
            pltpu.VMEM((SUBLANES, LRU_WIDTH), F32),
        ],
        compiler_params=_cparams("arbitrary", "arbitrary", "arbitrary"),
        name="lru",
    )(ab3, ab3, ab3, ab3, conv_w, conv_b, wa_bd, ba, wx_bd, bx, lam)


def _convmod_kernel(u_ref, g_ref, up_ref, gp_ref, un_ref, gn_ref, w_ref, b_ref, lg_ref, lb_ref,
                    o_ref, ext_ref, *, n_chunks):
    tc = SEQ_CHUNK
    j = pl.program_id(1)
    zero_halo = jnp.zeros((CONV_HALO, CONV_WIDTH), F32)
    glu = lambda u, g: u * jax.nn.sigmoid(g)
    ext_ref[0:CONV_HALO, :] = jnp.where(j == 0, zero_halo, glu(up_ref[...], gp_ref[...]))
    ext_ref[CONV_HALO:CONV_HALO + tc, :] = glu(u_ref[...], g_ref[...])
    ext_ref[CONV_HALO + tc:, :] = jnp.where(j == n_chunks - 1, zero_halo, glu(un_ref[...], gn_ref[...]))
    acc = b_ref[...]
    for k in range(CONV_KERNEL):
        off = CONV_HALO - CONV_KERNEL // 2 + k
        acc = acc + w_ref[k:k + 1, :] * ext_ref[off:off + tc, :]
    mu = jnp.mean(acc, axis=-1, keepdims=True)
    cen = acc - mu
    var = jnp.mean(cen * cen, axis=-1, keepdims=True)
    z = cen * lax.rsqrt(var + EPS) * lg_ref[...] + lb_ref[...]
    o_ref[...] = (z * jax.nn.sigmoid(z)).astype(BF16)


def _convmod(ab3, dw_w, dw_b, cln_g, cln_b):
    bsz, s, _ = ab3.shape
    tc = SEQ_CHUNK
    n = s // tc
    hb = tc // CONV_HALO
    w2 = lambda b, j: (0, 0)
    ucol = 2 * LRU_WIDTH // CONV_WIDTH
    gcol = ucol + 1
    cur = lambda col: pl.BlockSpec((None, tc, CONV_WIDTH), lambda b, j: (b, j, col))
    prev = lambda col: pl.BlockSpec((None, CONV_HALO, CONV_WIDTH),
                                    lambda b, j: (b, jnp.maximum(j * hb - 1, 0), col))
    nxt = lambda col: pl.BlockSpec((None, CONV_HALO, CONV_WIDTH),
                                   lambda b, j: (b, jnp.minimum((j + 1) * hb, s // CONV_HALO - 1), col))
    return pl.pallas_call(
        functools.partial(_convmod_kernel, n_chunks=n),
        out_shape=jax.ShapeDtypeStruct((bsz, s, CONV_WIDTH), BF16),
        grid=(bsz, n),
        in_specs=[
            cur(ucol), cur(gcol), prev(ucol), prev(gcol), nxt(ucol), nxt(gcol),
            pl.BlockSpec((CONV_KERNEL, CONV_WIDTH), w2),
            pl.BlockSpec((1, CONV_WIDTH), w2),
            pl.BlockSpec((1, CONV_WIDTH), w2),
            pl.BlockSpec((1, CONV_WIDTH), w2),
        ],
        out_specs=pl.BlockSpec((None, tc, CONV_WIDTH), lambda b, j: (b, j, 0)),
        scratch_shapes=[pltpu.VMEM((tc + 2 * CONV_HALO, CONV_WIDTH), F32)],
        compiler_params=_cparams("arbitrary", "arbitrary"),
        name="convmod",
    )(ab3, ab3, ab3, ab3, ab3, ab3, dw_w, dw_b, cln_g, cln_b)


def _attn_kernel(q_ref, k_ref, vt_ref, lv_ref, sg_ref, o_ref, *, lam_init):
    tq = Q_TILE
    q = q_ref[...]
    lane = lax.broadcasted_iota(jnp.int32, q.shape, 1)
    zero = jnp.zeros_like(q)
    qq = jnp.concatenate([jnp.where(lane < ATT_QKDIM, q, zero), jnp.where(lane >= ATT_QKDIM, q, zero)], axis=0)
    st = lax.dot_general(k_ref[...], qq, (((1,), (1,)), ((), ())), preferred_element_type=F32)
    e = jnp.exp(st - jnp.max(st, axis=0, keepdims=True))
    denom = jnp.sum(e, axis=0, keepdims=True)
    ovt = jnp.dot(vt_ref[...], e.astype(BF16), preferred_element_type=F32) / denom
    lv = lv_ref[...]
    lam = (jnp.exp(jnp.sum(lv[0:1, :] * lv[1:2, :], axis=-1, keepdims=True))
           - jnp.exp(jnp.sum(lv[2:3, :] * lv[3:4, :], axis=-1, keepdims=True)) + lam_init)
    ot = ovt[:, 0:tq] - lam * ovt[:, tq:]
    ot = ot * lax.rsqrt(jnp.mean(ot * ot, axis=0, keepdims=True) + EPS) * sg_ref[...] * (1.0 - lam_init)
    o_ref[...] = ot.T.astype(BF16)


def _attn(qk3, vt3, lam_vec, subln_g, lam_init):
    bsz, s, _ = qk3.shape
    tq = Q_TILE
    kcol = ATT_WIDTH // LANES
    w2 = lambda b, h, i: (0, 0)
    return pl.pallas_call(
        functools.partial(_attn_kernel, lam_init=lam_init),
        out_shape=jax.ShapeDtypeStruct((bsz, s, ATT_WIDTH), BF16),
        grid=(bsz, ATT_HEADS, s // tq),
        in_specs=[
            pl.BlockSpec((None, tq, LANES), lambda b, h, i: (b, i, h)),
            pl.BlockSpec((None, s, LANES), lambda b, h, i: (b, 0, kcol + h)),
            pl.BlockSpec((None, ATT_VDIM, s), lambda b, h, i: (b, h, 0)),
            pl.BlockSpec((4, ATT_QKDIM), w2),
            pl.BlockSpec((ATT_VDIM, 1), w2),
        ],
        out_specs=pl.BlockSpec((None, tq, ATT_VDIM), lambda b, h, i: (b, i, h)),
        compiler_params=_cparams("arbitrary", "arbitrary", "arbitrary"),
        name="attn",
    )(qk3, qk3, vt3, lam_vec, subln_g.reshape(ATT_VDIM, 1))


def _first_argmax(vals):
    best = vals[0]
    idx = jnp.zeros(best.shape, jnp.int32)
    for i in range(1, len(vals)):
        better = vals[i] > best
        idx = jnp.where(better, i, idx)
        best = jnp.where(better, vals[i], best)
    return idx, best


def _out_proj_kernel(x_ref, ya_ref, yb_ref, yc_ref, w_ref, mod_ref, g_ref, rw_ref, rb_ref,
                     xo_ref, hp_ref, meta_ref, cnt_ref):
    tm = TOKEN_TILE
    y = jnp.dot(ya_ref[...], w_ref[0:LRU_WIDTH, :], preferred_element_type=F32)
    y = y + jnp.dot(yb_ref[...], w_ref[LRU_WIDTH:LRU_WIDTH + CONV_WIDTH, :], preferred_element_type=F32)
    y = y + jnp.dot(yc_ref[...], w_ref[LRU_WIDTH + CONV_WIDTH:, :], preferred_element_type=F32)
    x = x_ref[...] + mod_ref[2:3, :] * y
    xo_ref[...] = x
    h = _rms(x, g_ref[...]) * (1.0 + mod_ref[4:5, :]) + mod_ref[3:4, :]
    hp_ref[:, 0:D_MODEL] = h

    logits = lax.dot_general(rw_ref[...], h, (((1,), (1,)), ((), ())), precision=lax.Precision.HIGHEST,
                             preferred_element_type=F32) + rb_ref[...]
    gl = [logits[g:g + 1, :] for g in range(N_GROUPS)]
    g_idx, g_max = _first_argmax(gl)
    denom = gl[0] * 0.0
    for g in range(N_GROUPS):
        denom = denom + jnp.exp(gl[g] - g_max)
    g_w = 1.0 / denom
    fl = []
    for e in range(EXPERTS_PER_GROUP):
        sel = logits[N_GROUPS + e:N_GROUPS + e + 1, :]
        for g in range(1, N_GROUPS):
            r0 = N_GROUPS + g * EXPERTS_PER_GROUP + e
            sel = jnp.where(g_idx == g, logits[r0:r0 + 1, :], sel)
        fl.append(sel)
    i1, v1 = _first_argmax(fl)
    neg = jnp.full(v1.shape, -jnp.inf, F32)
    i2, v2 = _first_argmax([jnp.where(i1 == e, neg, fl[e]) for e in range(EXPERTS_PER_GROUP)])
    t = jnp.exp(v2 - v1)
    p1 = g_w / (1.0 + t)
    p2 = g_w * t / (1.0 + t)
    first_lo = i1 < i2
    lo = jnp.where(first_lo, i1, i2)
    hi = jnp.where(first_lo, i2, i1)
    pair = jnp.where(lo == 0, 0, jnp.where(lo == 1, 3, 5)) + hi - lo - 1
    bucket = g_idx * PAIRS_PER_GROUP + pair
    p_lo = jnp.where(first_lo, p1, p2)
    p_hi = jnp.where(first_lo, p2, p1)
    wrows = jnp.concatenate([p_lo, p_hi, jnp.zeros((LANES - 2, tm), F32)], axis=0)
    hp_ref[:, D_MODEL:] = wrows.T

    @pl.when(pl.program_id(0) == 0)
    def _():
        cnt_ref[...] = jnp.zeros_like(cnt_ref)

    onehot = lax.broadcasted_iota(jnp.int32, (BUCKET_ROWS, tm), 0) == bucket
    before = lax.broadcasted_iota(jnp.int32, (tm, tm), 0) < lax.broadcasted_iota(jnp.int32, (tm, tm), 1)
    prefix = jnp.dot(jnp.where(onehot, 1.0, 0.0).astype(BF16), jnp.where(before, 1.0, 0.0).astype(BF16),
                     preferred_element_type=F32)
    base = cnt_ref[:, 0:1]
    rank = jnp.sum(jnp.where(onehot, prefix + base, 0.0), axis=0, keepdims=True)
    cnt_ref[...] = cnt_ref[...] + jnp.sum(jnp.where(onehot, 1.0, 0.0), axis=1, keepdims=True)
    meta_ref[...] = jnp.concatenate([bucket.astype(F32), rank, jnp.zeros((SUBLANES - 2, tm), F32)], axis=0)


def _out_proj(x, ya, yb, yc, w_out_bf, mod, g2, rw_t, rb, s):
    t = x.shape[0]
    tm = TOKEN_TILE
    per_seq = s // tm
    row = lambda i: (i, 0)
    w2 = lambda i: (0, 0)
    return pl.pallas_call(
        _out_proj_kernel,
        out_shape=(
            jax.ShapeDtypeStruct((t, D_MODEL), F32),
            jax.ShapeDtypeStruct((t, PAYLOAD_COLS), F32),
            jax.ShapeDtypeStruct((SUBLANES, t), F32),
            jax.ShapeDtypeStruct((BUCKET_ROWS, LANES), F32),
        ),
        grid=(t // tm,),
        in_specs=[
            pl.BlockSpec((tm, D_MODEL), row),
            pl.BlockSpec((tm, LRU_WIDTH), row),
            pl.BlockSpec((tm, CONV_WIDTH), row),
            pl.BlockSpec((tm, ATT_WIDTH), row),
            pl.BlockSpec((D_MODEL, D_MODEL), w2),
            pl.BlockSpec((None, 6, D_MODEL), lambda i: (i // per_seq, 0, 0)),
            pl.BlockSpec((1, D_MODEL), w2),
            pl.BlockSpec((ROUTER_ROWS, D_MODEL), w2),
            pl.BlockSpec((ROUTER_ROWS, 1), w2),
        ],
        out_specs=(
            pl.BlockSpec((tm, D_MODEL), row),
            pl.BlockSpec((tm, PAYLOAD_COLS), row),
            pl.BlockSpec((SUBLANES, tm), lambda i: (0, i)),
            pl.BlockSpec((BUCKET_ROWS, LANES), w2),
        ),
        compiler_params=_cparams("arbitrary"),
        name="out_proj",
    )(x, ya, yb, yc, w_out_bf, mod, g2, rw_t, rb)


def _route_tables(meta, cnt, t):
    tm = TOKEN_TILE
    n_tiles = t // tm + N_BUCKETS
    counts = cnt[:N_BUCKETS, 0].astype(jnp.int32)
    tiles_per = (counts + tm - 1) // tm
    tile_end = jnp.cumsum(tiles_per)
    offs = (tile_end - tiles_per) * tm
    n_used = tile_end[-1:]
    bucket = meta[0].astype(jnp.int32)
    rank = meta[1].astype(jnp.int32)
    ids = jnp.arange(N_BUCKETS, dtype=jnp.int32)
    dest = rank + jnp.sum(jnp.where(bucket[:, None] == ids[None, :], offs[None, :], 0), axis=1)
    tile = jnp.minimum(jnp.arange(n_tiles, dtype=jnp.int32), n_used - 1)
    tb = jnp.sum((tile[:, None] >= tile_end[None, :]).astype(jnp.int32), axis=1)
    pair_lo = jnp.array([0, 0, 0, 1, 1, 2], jnp.int32)
    pair_hi = jnp.array([1, 2, 3, 2, 3, 3], jnp.int32)
    onehot_pair = (tb % PAIRS_PER_GROUP)[:, None] == jnp.arange(PAIRS_PER_GROUP, dtype=jnp.int32)[None, :]
    e_lo = (tb // PAIRS_PER_GROUP) * EXPERTS_PER_GROUP + jnp.sum(jnp.where(onehot_pair, pair_lo[None, :], 0), axis=1)
    e_hi = (tb // PAIRS_PER_GROUP) * EXPERTS_PER_GROUP + jnp.sum(jnp.where(onehot_pair, pair_hi[None, :], 0), axis=1)
    return dest, jnp.stack([e_lo, e_hi], axis=1).reshape(-1), n_used, n_tiles


def _row_copy(src_ref, src_row, dst_ref, dst_row, sem):
    return pltpu.make_async_copy(src_ref.at[pl.ds(src_row, 1), :], dst_ref.at[pl.ds(dst_row, 1), :], sem)


def _dispatch_kernel(dest_ref, hp_ref, xs_in_ref, xs_ref, sem):
    del xs_in_ref
    tm = TOKEN_TILE
    base = pl.program_id(0) * tm

    def issue(r, carry):
        _row_copy(hp_ref, r, xs_ref, dest_ref[base + r], sem).start()
        return carry

    lax.fori_loop(0, tm, issue, 0, unroll=8)
    pltpu.make_async_copy(hp_ref, xs_ref.at[pl.ds(0, tm), :], sem).wait()


def _dispatch(dest, hp, n_rows):
    t = hp.shape[0]
    tm = TOKEN_TILE
    xs0 = jnp.zeros((n_rows, PAYLOAD_COLS), F32)
    return pl.pallas_call(
        _dispatch_kernel,
        out_shape=jax.ShapeDtypeStruct((n_rows, PAYLOAD_COLS), F32),
        grid_spec=pltpu.PrefetchScalarGridSpec(
            num_scalar_prefetch=1,
            grid=(t // tm,),
            in_specs=[
                pl.BlockSpec((tm, PAYLOAD_COLS), lambda i, dest: (i, 0)),
                pl.BlockSpec(memory_space=pl.ANY),
            ],
            out_specs=pl.BlockSpec(memory_space=pl.ANY),
            scratch_shapes=[pltpu.SemaphoreType.DMA],
        ),
        input_output_aliases={2: 0},
        compiler_params=_cparams("arbitrary"),
        name="dispatch",
    )(dest, hp, xs0)


def _experts_kernel(eid_ref, nused_ref, x_ref, w13a_ref, w2a_ref, w13b_ref, w2b_ref, o_ref):
    del eid_ref
    busy = pl.program_id(0) < nused_ref[0]

    @pl.when(jnp.logical_not(busy))
    def _():
        o_ref[...] = jnp.zeros_like(o_ref)

    @pl.when(busy)
    def _():
        x = x_ref[:, 0:D_MODEL].astype(BF16)

        def expert(w13_ref, w2_ref):
            h13 = jnp.dot(x, w13_ref[...], preferred_element_type=F32)
            a = h13[:, 0:D_EXPERT]
            he = (a * jax.nn.sigmoid(a)) * h13[:, D_EXPERT:]
            return jnp.dot(he.astype(BF16), w2_ref[...], preferred_element_type=F32)

        o_ref[...] = (x_ref[:, D_MODEL:D_MODEL + 1] * expert(w13a_ref, w2a_ref)
                      + x_ref[:, D_MODEL + 1:D_MODEL + 2] * expert(w13b_ref, w2b_ref))


def _experts(eids, n_used, xs, w13, w2):
    n_rows = xs.shape[0]
    tm = TOKEN_TILE
    rows = lambda i, eid, nu: (jnp.minimum(i, nu[0] - 1), 0)
    wspec = lambda shape, slot: pl.BlockSpec((None,) + shape, lambda i, eid, nu: (eid[2 * i + slot], 0, 0))
    return pl.pallas_call(
        _experts_kernel,
        out_shape=jax.ShapeDtypeStruct((n_rows, D_MODEL), F32),
        grid_spec=pltpu.PrefetchScalarGridSpec(
            num_scalar_prefetch=2,
            grid=(n_rows // tm,),
            in_specs=[
                pl.BlockSpec((tm, PAYLOAD_COLS), rows),
                wspec((D_MODEL, 2 * D_EXPERT), 0),
                wspec((D_EXPERT, D_MODEL), 0),
                wspec((D_MODEL, 2 * D_EXPERT), 1),
                wspec((D_EXPERT, D_MODEL), 1),
            ],
            out_specs=pl.BlockSpec((tm, D_MODEL), lambda i, eid, nu: (i, 0)),
        ),
        compiler_params=_cparams("arbitrary"),
        name="experts",
    )(eids, n_used, xs, w13, w2, w13, w2)


def _combine_kernel(dest_ref, x_ref, ys_ref, mod_ref, g_ref, o_ref, ybuf_ref, sem, *, final):
    tm = TOKEN_TILE
    base = pl.program_id(0) * tm

    def issue(r, carry):
        _row_copy(ys_ref, dest_ref[base + r], ybuf_ref, r, sem).start()
        return carry

    lax.fori_loop(0, tm, issue, 0, unroll=8)
    pltpu.make_async_copy(ys_ref.at[pl.ds(0, tm), :], ybuf_ref, sem).wait()
    x = x_ref[...] + mod_ref[5:6, :] * ybuf_ref[...]
    o_ref[...] = _rms(x, g_ref[...]) if final else x


def _combine(dest, x, ys, mod, final_g, s, final):
    t = x.shape[0]
    tm = TOKEN_TILE
    per_seq = s // tm
    return pl.pallas_call(
        functools.partial(_combine_kernel, final=final),
        out_shape=jax.ShapeDtypeStruct((t, D_MODEL), F32),
        grid_spec=pltpu.PrefetchScalarGridSpec(
            num_scalar_prefetch=1,
            grid=(t // tm,),
            in_specs=[
                pl.BlockSpec((tm, D_MODEL), lambda i, dest: (i, 0)),
                pl.BlockSpec(memory_space=pl.ANY),
                pl.BlockSpec((None, 6, D_MODEL), lambda i, dest: (i // per_seq, 0, 0)),
                pl.BlockSpec((1, D_MODEL), lambda i, dest: (0, 0)),
            ],
            out_specs=pl.BlockSpec((tm, D_MODEL), lambda i, dest: (i, 0)),
            scratch_shapes=[pltpu.VMEM((tm, D_MODEL), F32), pltpu.SemaphoreType.DMA],
        ),
        compiler_params=_cparams("arbitrary"),
        name="combine",
    )(dest, x, ys, mod, final_g)


def _block_diag(w):
    h, dh, _ = w.shape
    eye = jnp.eye(h, dtype=w.dtype)
    return (eye[:, None, :, None] * w[:, :, None, :]).reshape(h * dh, h * dh)


def _prep_layer(l, p):
    row = lambda a: a.reshape(1, -1)
    rw = jnp.concatenate([p["router_w1"][l], p["router_w2"][l].reshape(D_MODEL, N_EXPERTS)], axis=1)
    rw = jnp.pad(rw, ((0, 0), (0, ROUTER_ROWS - rw.shape[1])))
    rb = jnp.concatenate([p["router_b1"][l], p["router_b2"][l].reshape(N_EXPERTS)])
    rb = jnp.pad(rb, (0, ROUTER_ROWS - rb.shape[0]))
    return dict(
        norm1_g=row(p["norm1_g"][l]),
        norm2_g=row(p["norm2_g"][l]),
        w_in=p["w_in"][l].astype(BF16),
        w_out=p["w_out"][l].astype(BF16),
        conv_w=p["conv_w"][l],
        conv_b=row(p["conv_b"][l]),
        wa_bd=jnp.stack([_block_diag(p["lru_wa"][l, d]) for d in range(2)]).astype(BF16),
        wx_bd=jnp.stack([_block_diag(p["lru_wx"][l, d]) for d in range(2)]).astype(BF16),
        ba=p["lru_ba"][l].reshape(2, 1, LRU_WIDTH),
        bx=p["lru_bx"][l].reshape(2, 1, LRU_WIDTH),
        lam=p["lru_lambda"][l].reshape(2, 1, LRU_WIDTH),
        dw_w=p["dw_w"][l],
        dw_b=row(p["dw_b"][l]),
        cln_g=row(p["cln_g"][l]),
        cln_b=row(p["cln_b"][l]),
        lam_vec=p["lam_vec"][l],
        subln_g=row(p["subln_g"][l]),
        rw_t=rw.T,
        rb=rb.reshape(ROUTER_ROWS, 1),
        w13=jnp.concatenate([p["moe_w1"][l], p["moe_w3"][l]], axis=-1).astype(BF16),
        w2=p["moe_w2"][l].astype(BF16),
    )


def _trunk(x, mods, layers, final_g):
    bsz, s, _ = x.shape
    t = bsz * s
    tables = _rope_tables(s)
    x = x.reshape(t, D_MODEL)
    for l, w in enumerate(layers):
        mod = mods[l]
        ab, qk, vt = _in_proj(x, mod, w["norm1_g"], w["w_in"], tables, s)
        ab3 = ab.reshape(bsz, s, AB_COLS)
        ya = _lru(ab3, w["conv_w"], w["conv_b"], w["wa_bd"], w["ba"], w["wx_bd"], w["bx"], w["lam"])
        yb = _convmod(ab3, w["dw_w"], w["dw_b"], w["cln_g"], w["cln_b"])
        lam_init = 0.8 - 0.6 * math.exp(-0.3 * l)
        yc = _attn(qk.reshape(bsz, s, QK_COLS), vt, w["lam_vec"], w["subln_g"], lam_init)
        x, hp, meta, cnt = _out_proj(x, ya.reshape(t, LRU_WIDTH), yb.reshape(t, CONV_WIDTH),
                                     yc.reshape(t, ATT_WIDTH), w["w_out"], mod, w["norm2_g"], w["rw_t"], w["rb"], s)
        dest, eids, n_used, n_tiles = _route_tables(meta, cnt, t)
        xs = _dispatch(dest, hp, n_tiles * TOKEN_TILE)
        ys = _experts(eids, n_used, xs, w["w13"], w["w2"])
        x = _combine(dest, x, ys, mod, final_g, s, final=(l == DEPTH - 1))
    return x.reshape(bsz, s, D_MODEL)


def kernel(x_prompt, x_sample, c_prompt, c_sample, norm1_g, norm2_g, final_g, ada_w, ada_b, w_in, w_out, conv_w, conv_b, lru_wa, lru_ba, lru_wx, lru_bx, lru_lambda, dw_w, dw_b, cln_g, cln_b, lam_vec, subln_g, router_w1, router_b1, router_w2, router_b2, moe_w1, moe_w3, moe_w2):
    p = dict(norm1_g=norm1_g, norm2_g=norm2_g, w_in=w_in, w_out=w_out, conv_w=conv_w, conv_b=conv_b,
             lru_wa=lru_wa, lru_ba=lru_ba, lru_wx=lru_wx, lru_bx=lru_bx, lru_lambda=lru_lambda,
             dw_w=dw_w, dw_b=dw_b, cln_g=cln_g, cln_b=cln_b, lam_vec=lam_vec, subln_g=subln_g,
             router_w1=router_w1, router_b1=router_b1, router_w2=router_w2, router_b2=router_b2,
             moe_w1=moe_w1, moe_w3=moe_w3, moe_w2=moe_w2)
    layers = [_prep_layer(l, p) for l in range(DEPTH)]
    nb = c_prompt.shape[0]
    mods = _ada_mod(jnp.concatenate([c_prompt, c_sample], axis=0), ada_w, ada_b)
    fg = final_g.reshape(1, D_MODEL)
    y_prompt = _trunk(x_prompt, mods[:, :nb], layers, fg)
    y_sample = _trunk(x_sample, mods[:, nb:], layers, fg)
    return (y_prompt, y_sample)
```

```python
import functools
import math

import jax
import jax.numpy as jnp
from jax import lax
from jax.experimental import pallas as pl
from jax.experimental.pallas import tpu as pltpu

F32 = jnp.float32
BF16 = jnp.bfloat16

D_MODEL = 1024
DEPTH = 2
LRU_WIDTH = 256
LRU_HEADS = 4
LRU_CONV = 4
LRU_C = 8.0
CONV_WIDTH = 256
CONV_KERNEL = 31
ATT_WIDTH = 512
ATT_HEADS = 4
ATT_VDIM = 128
ATT_QKDIM = 64
ROPE_DIM = 16
ROPE_THETA = 500000.0
AB_COLS = 2 * LRU_WIDTH + 2 * CONV_WIDTH
QK_COLS = 2 * ATT_WIDTH
IN_COLS = AB_COLS + QK_COLS + ATT_WIDTH
N_GROUPS = 4
EXPERTS_PER_GROUP = 4
N_EXPERTS = 16
D_EXPERT = 512
EPS = 1e-6

LANES = 128
SUBLANES = 8
ROUTER_ROWS = 24
PAIRS_PER_GROUP = 6
N_BUCKETS = N_GROUPS * PAIRS_PER_GROUP
BUCKET_ROWS = 32
PAYLOAD_COLS = D_MODEL + LANES
VMEM_LIMIT = 56 * 1024 * 1024

TOKEN_TILE = 512
SEQ_CHUNK = 512
SCORE_TILE_BYTES = 8 * 1024 * 1024
MAX_Q_TILE = 512
ONES_ROWS = 16
KEY_CHUNK = 256
LRU_HALO = 8
CONV_HALO = 16


def _cparams(*sem):
    return pltpu.CompilerParams(dimension_semantics=sem, vmem_limit_bytes=VMEM_LIMIT)


def _rms(x, g):
    return x * lax.rsqrt(jnp.mean(x * x, axis=-1, keepdims=True) + EPS) * g


def _ada_kernel(c_ref, w_ref, b_ref, o_ref):
    c = c_ref[...]
    cs = (c * jax.nn.sigmoid(c)).astype(BF16)
    o_ref[...] = jnp.dot(cs, w_ref[...].astype(BF16), preferred_element_type=F32) + b_ref[...]


def _ada_mod(c, ada_w, ada_b):
    nb = c.shape[0]
    tn = 1536
    out = pl.pallas_call(
        _ada_kernel,
        out_shape=jax.ShapeDtypeStruct((DEPTH, nb, 6 * D_MODEL), F32),
        grid=(DEPTH, 6 * D_MODEL // tn),
        in_specs=[
            pl.BlockSpec((nb, D_MODEL), lambda l, j: (0, 0)),
            pl.BlockSpec((None, D_MODEL, tn), lambda l, j: (l, 0, j)),
            pl.BlockSpec((None, 1, tn), lambda l, j: (l, 0, j)),
        ],
        out_specs=pl.BlockSpec((None, nb, tn), lambda l, j: (l, 0, j)),
        compiler_params=_cparams("arbitrary", "arbitrary"),
        name="ada_mod",
    )(c, ada_w, ada_b.reshape(DEPTH, 1, 6 * D_MODEL))
    return out.reshape(DEPTH, nb, 6, D_MODEL)


def _in_proj_kernel(x_ref, mod_ref, g_ref, w_ref, cos_ref, sa_ref, sb_ref, ab_ref, qk_ref, vt_ref):
    x = x_ref[...]
    h = _rms(x, g_ref[...]) * (1.0 + mod_ref[1:2, :]) + mod_ref[0:1, :]
    hb = h.astype(BF16)
    ab_ref[...] = jnp.dot(hb, w_ref[:, 0:AB_COLS], preferred_element_type=F32)
    qk = jnp.dot(hb, w_ref[:, AB_COLS:AB_COLS + QK_COLS], preferred_element_type=F32)
    cos, sa, sb = cos_ref[...], sa_ref[...], sb_ref[...]
    for c in range(QK_COLS // LANES):
        blk = qk[:, c * LANES:(c + 1) * LANES]
        rot = (blk * cos + pltpu.roll(blk, ROPE_DIM // 2, 1) * sa
               + pltpu.roll(blk, LANES - ROPE_DIM // 2, 1) * sb)
        if c < ATT_WIDTH // LANES:
            rot = rot * (ATT_QKDIM ** -0.5)
        qk_ref[:, c * LANES:(c + 1) * LANES] = rot.astype(BF16)
    v = jnp.dot(hb, w_ref[:, AB_COLS + QK_COLS:], preferred_element_type=F32)
    vt_ref[...] = v.T.astype(BF16)


def _rope_tables(s):
    half = ROPE_DIM // 2
    inv = ROPE_THETA ** (-(jnp.arange(half, dtype=F32) * 2.0) / ROPE_DIM)
    ang = jnp.arange(s, dtype=jnp.int32).astype(F32)[:, None] * inv[None, :]
    cos, sin = jnp.cos(ang), jnp.sin(ang)
    rest = ATT_QKDIM - ROPE_DIM
    one = jnp.ones((s, rest), F32)
    zero = jnp.zeros((s, rest), F32)
    zh = jnp.zeros((s, half), F32)
    reps = LANES // ATT_QKDIM
    cos_t = jnp.tile(jnp.concatenate([cos, cos, one], axis=1), (1, reps))
    sa_t = jnp.tile(jnp.concatenate([zh, sin, zero], axis=1), (1, reps))
    sb_t = jnp.tile(jnp.concatenate([-sin, zh, zero], axis=1), (1, reps))
    return cos_t, sa_t, sb_t


def _in_proj(x, mod, g, w_in_bf, tables, s):
    t = x.shape[0]
    tm = TOKEN_TILE
    per_seq = s // tm
    row = lambda i: (i, 0)
    pos = lambda i: (i % per_seq, 0)
    return pl.pallas_call(
        _in_proj_kernel,
        out_shape=(
            jax.ShapeDtypeStruct((t, AB_COLS), F32),
            jax.ShapeDtypeStruct((t, QK_COLS), BF16),
            jax.ShapeDtypeStruct((t // s, ATT_WIDTH, s), BF16),
        ),
        grid=(t // tm,),
        in_specs=[
            pl.BlockSpec((tm, D_MODEL), row),
            pl.BlockSpec((None, 6, D_MODEL), lambda i: (i // per_seq, 0, 0)),
            pl.BlockSpec((1, D_MODEL), lambda i: (0, 0)),
            pl.BlockSpec((D_MODEL, IN_COLS), lambda i: (0, 0)),
            pl.BlockSpec((tm, LANES), pos),
            pl.BlockSpec((tm, LANES), pos),
            pl.BlockSpec((tm, LANES), pos),
        ],
        out_specs=(
            pl.BlockSpec((tm, AB_COLS), row),
            pl.BlockSpec((tm, QK_COLS), row),
            pl.BlockSpec((None, ATT_WIDTH, tm), lambda i: (i // per_seq, 0, i % per_seq)),
        ),
        compiler_params=_cparams("arbitrary"),
        name="in_proj",
    )(x, mod, g, w_in_bf, *tables)


def _neg_expm1(y):
    series = -y * (1.0 + y * (1.0 / 2.0) * (1.0 + y * (1.0 / 3.0) * (1.0 + y * (1.0 / 4.0) * (1.0 + y * (1.0 / 5.0)))))
    return jnp.where(y > -1.0 / 64.0, series, 1.0 - jnp.exp(y))


def _tile_scan(a, b, reverse):
    n = a.shape[0]
    a3 = a.reshape(n // SUBLANES, SUBLANES, a.shape[1])
    b3 = b.reshape(n // SUBLANES, SUBLANES, b.shape[1])
    row = lax.broadcasted_iota(jnp.int32, a3.shape, 1)
    for d in (1, 2, 4):
        shift = SUBLANES - d if reverse else d
        a_s = pltpu.roll(a3, shift, 1)
        b_s = pltpu.roll(b3, shift, 1)
        m = (row < SUBLANES - d) if reverse else (row >= d)
        b3 = jnp.where(m, b3 + a3 * b_s, b3)
        a3 = jnp.where(m, a3 * a_s, a3)
    return a3.reshape(a.shape), b3.reshape(b.shape)


def _lru_kernel(cur_ref, prev_ref, next_ref, ga_ref, cw_ref, cb_ref, wa_ref, ba_ref, wx_ref, bx_ref,
                lam_ref, o_ref, hf_ref, ext_ref, a_ref, b_ref, h_ref, carry_ref, *, n_chunks):
    tc = SEQ_CHUNK
    p = pl.program_id(1)
    j = pl.program_id(2)
    jx = j + p * (n_chunks - 1 - 2 * j)

    @pl.when(j == 0)
    def _():
        carry_ref[...] = jnp.zeros_like(carry_ref)

    zero_halo = jnp.zeros((LRU_HALO, LRU_WIDTH), F32)
    ext_ref[0:LRU_HALO, :] = jnp.where(jx == 0, zero_halo, prev_ref[...])
    ext_ref[LRU_HALO:LRU_HALO + tc, :] = cur_ref[...]
    ext_ref[LRU_HALO + tc:, :] = jnp.where(jx == n_chunks - 1, zero_halo, next_ref[...])
    xc = cb_ref[...]
    for k in range(LRU_CONV):
        off = LRU_HALO - 2 + k
        xc = xc + cw_ref[k:k + 1, :] * ext_ref[off:off + tc, :]
    xb = xc.astype(BF16)
    r = jax.nn.sigmoid(jnp.dot(xb, wa_ref[...], preferred_element_type=F32) + ba_ref[...])
    gate_i = jax.nn.sigmoid(jnp.dot(xb, wx_ref[...], preferred_element_type=F32) + bx_ref[...])
    log_a = (-LRU_C) * r * jax.nn.softplus(-lam_ref[...])
    a = jnp.exp(log_a)
    b = jnp.sqrt(_neg_expm1(2.0 * log_a)) * (gate_i * xc)

    n_tiles = tc // SUBLANES

    def run(reverse):
        a_s, b_s = _tile_scan(a, b, reverse)
        a_ref[...] = a_s
        b_ref[...] = b_s

        def body(i, hc):
            ti = (n_tiles - 1 - i) if reverse else i
            off = pl.multiple_of(ti * SUBLANES, SUBLANES)
            h = b_ref[pl.ds(off, SUBLANES), :] + a_ref[pl.ds(off, SUBLANES), :] * hc
            h_ref[pl.ds(off, SUBLANES), :] = h
            edge = h[0:1, :] if reverse else h[SUBLANES - 1:SUBLANES, :]
            return jnp.broadcast_to(edge, (SUBLANES, LRU_WIDTH))

        carry_ref[...] = lax.fori_loop(0, n_tiles, body, carry_ref[...], unroll=8)

    row0 = pl.multiple_of(jx * tc, tc)

    @pl.when(p == 0)
    def _():
        run(False)
        hf_ref[pl.ds(row0, tc), :] = h_ref[...]

    @pl.when(p == 1)
    def _():
        run(True)
        hsum = hf_ref[pl.ds(row0, tc), :] + h_ref[...]
        o_ref[...] = (jax.nn.gelu(ga_ref[...]) * hsum).astype(BF16)


def _lru(ab3, conv_w, conv_b, wa_bd, ba, wx_bd, bx, lam):
    bsz, s, _ = ab3.shape
    tc = SEQ_CHUNK
    n = s // tc
    hb = tc // LRU_HALO
    jx = lambda p, j: j + p * (n - 1 - 2 * j)
    jg = lambda p, j: n - 1 - p * j
    w2 = lambda b, p, j: (0, 0)
    wdir = lambda b, p, j: (p, 0, 0)
    return pl.pallas_call(
        functools.partial(_lru_kernel, n_chunks=n),
        out_shape=jax.ShapeDtypeStruct((bsz, s, LRU_WIDTH), BF16),
        grid=(bsz, 2, n),
        in_specs=[
            pl.BlockSpec((None, tc, LRU_WIDTH), lambda b, p, j: (b, jx(p, j), 0)),
            pl.BlockSpec((None, LRU_HALO, LRU_WIDTH),
                         lambda b, p, j: (b, jnp.maximum(jx(p, j) * hb - 1, 0), 0)),
            pl.BlockSpec((None, LRU_HALO, LRU_WIDTH),
                         lambda b, p, j: (b, jnp.minimum((jx(p, j) + 1) * hb, s // LRU_HALO - 1), 0)),
            pl.BlockSpec((None, tc, LRU_WIDTH), lambda b, p, j: (b, jg(p, j), 1)),
            pl.BlockSpec((LRU_CONV, LRU_WIDTH), w2),
            pl.BlockSpec((1, LRU_WIDTH), w2),
            pl.BlockSpec((None, LRU_WIDTH, LRU_WIDTH), wdir),
            pl.BlockSpec((None, 1, LRU_WIDTH), wdir),
            pl.BlockSpec((None, LRU_WIDTH, LRU_WIDTH), wdir),
            pl.BlockSpec((None, 1, LRU_WIDTH), wdir),
            pl.BlockSpec((None, 1, LRU_WIDTH), wdir),
        ],
        out_specs=pl.BlockSpec((None, tc, LRU_WIDTH), lambda b, p, j: (b, jg(p, j), 0)),
        scratch_shapes=[
            pltpu.VMEM((s, LRU_WIDTH), F32),
            pltpu.VMEM((tc + 2 * LRU_HALO, LRU_WIDTH), F32),
            pltpu.VMEM((tc, LRU_WIDTH), F32),
            pltpu.VMEM((tc, LRU_WIDTH), F32),
            pltpu.VMEM((tc, LRU_WIDTH), F32),
            pltpu.VMEM((SUBLANES, LRU_WIDTH), F32),
        ],
        compiler_params=_cparams("arbitrary", "arbitrary", "arbitrary"),
        name="lru",
    )(ab3, ab3, ab3, ab3, conv_w, conv_b, wa_bd, ba, wx_bd, bx, lam)


def _convmod_kernel(u_ref, g_ref, up_ref, gp_ref, un_ref, gn_ref, w_ref, b_ref, lg_ref, lb_ref,
                    o_ref, ext_ref, *, n_chunks):
    tc = SEQ_CHUNK
    j = pl.program_id(1)
    zero_halo = jnp.zeros((CONV_HALO, CONV_WIDTH), F32)
    glu = lambda u, g: u * jax.nn.sigmoid(g)
    ext_ref[0:CONV_HALO, :] = jnp.where(j == 0, zero_halo, glu(up_ref[...], gp_ref[...]))
    ext_ref[CONV_HALO:CONV_HALO + tc, :] = glu(u_ref[...], g_ref[...])
    ext_ref[CONV_HALO + tc:, :] = jnp.where(j == n_chunks - 1, zero_halo, glu(un_ref[...], gn_ref[...]))
    acc = b_ref[...]
    for k in range(CONV_KERNEL):
        off = CONV_HALO - CONV_KERNEL // 2 + k
        acc = acc + w_ref[k:k + 1, :] * ext_ref[off:off + tc, :]
    mu = jnp.mean(acc, axis=-1, keepdims=True)
    cen = acc - mu
    var = jnp.mean(cen * cen, axis=-1, keepdims=True)
    z = cen * lax.rsqrt(var + EPS) * lg_ref[...] + lb_ref[...]
    o_ref[...] = (z * jax.nn.sigmoid(z)).astype(BF16)


def _convmod(ab3, dw_w, dw_b, cln_g, cln_b):
    bsz, s, _ = ab3.shape
    tc = SEQ_CHUNK
    n = s // tc
    hb = tc // CONV_HALO
    w2 = lambda b, j: (0, 0)
    ucol = 2 * LRU_WIDTH // CONV_WIDTH
    gcol = ucol + 1
    cur = lambda col: pl.BlockSpec((None, tc, CONV_WIDTH), lambda b, j: (b, j, col))
    prev = lambda col: pl.BlockSpec((None, CONV_HALO, CONV_WIDTH),
                                    lambda b, j: (b, jnp.maximum(j * hb - 1, 0), col))
    nxt = lambda col: pl.BlockSpec((None, CONV_HALO, CONV_WIDTH),
                                   lambda b, j: (b, jnp.minimum((j + 1) * hb, s // CONV_HALO - 1), col))
    return pl.pallas_call(
        functools.partial(_convmod_kernel, n_chunks=n),
        out_shape=jax.ShapeDtypeStruct((bsz, s, CONV_WIDTH), BF16),
        grid=(bsz, n),
        in_specs=[
            cur(ucol), cur(gcol), prev(ucol), prev(gcol), nxt(ucol), nxt(gcol),
            pl.BlockSpec((CONV_KERNEL, CONV_WIDTH), w2),
            pl.BlockSpec((1, CONV_WIDTH), w2),
            pl.BlockSpec((1, CONV_WIDTH), w2),
            pl.BlockSpec((1, CONV_WIDTH), w2),
        ],
        out_specs=pl.BlockSpec((None, tc, CONV_WIDTH), lambda b, j: (b, j, 0)),
        scratch_shapes=[pltpu.VMEM((tc + 2 * CONV_HALO, CONV_WIDTH), F32)],
        compiler_params=_cparams("arbitrary", "arbitrary"),
        name="convmod",
    )(ab3, ab3, ab3, ab3, ab3, ab3, dw_w, dw_b, cln_g, cln_b)


def _attn_kernel(q_ref, k_ref, vt_ref, lv_ref, sg_ref, o_ref, st0_ref, st1_ref, m0_ref, m1_ref, *, lam_init):
    tq = q_ref.shape[0]
    g = pl.program_id(0)

    @pl.when(g == 0)
    def _():
        st1_ref[...] = jnp.zeros_like(st1_ref)
        m1_ref[...] = jnp.zeros_like(m1_ref)

    def step(st_new_ref, m_new_ref, st_old_ref, m_old_ref):
        q = q_ref[...]
        lane = lax.broadcasted_iota(jnp.int32, q.shape, 1)
        zero = jnp.zeros_like(q)
        qq = jnp.concatenate([jnp.where(lane < ATT_QKDIM, q, zero), jnp.where(lane >= ATT_QKDIM, q, zero)], axis=0)
        kc = KEY_CHUNK
        n_chunks = k_ref.shape[0] // kc
        ones = jnp.ones((ONES_ROWS, kc), BF16)
        m_old = m_old_ref[...]
        chunk = lambda c: slice(c * kc, (c + 1) * kc)

        def scores(c, m_new):
            st_c = lax.dot_general(k_ref[chunk(c), :], qq, (((1,), (1,)), ((), ())), preferred_element_type=F32)
            st_new_ref[chunk(c), :] = st_c
            m_c = jnp.max(st_c, axis=0, keepdims=True)
            return m_c if m_new is None else jnp.maximum(m_new, m_c)

        m_new = scores(0, None)
        ovt = None
        for c in range(n_chunks):
            e_c = jnp.exp(st_old_ref[chunk(c), :] - m_old).astype(BF16)
            if c + 1 < n_chunks:
                m_new = scores(c + 1, m_new)
            vt_c = jnp.concatenate([vt_ref[:, chunk(c)], ones], axis=0)
            o_c = jnp.dot(vt_c, e_c, preferred_element_type=F32)
            ovt = o_c if ovt is None else ovt + o_c
        m_new_ref[...] = m_new
        ovt = ovt[0:ATT_VDIM, :] / ovt[ATT_VDIM:ATT_VDIM + 1, :]
        lv = lv_ref[...]
        lam = (jnp.exp(jnp.sum(lv[0:1, :] * lv[1:2, :], axis=-1, keepdims=True))
               - jnp.exp(jnp.sum(lv[2:3, :] * lv[3:4, :], axis=-1, keepdims=True)) + lam_init)
        ot = ovt[:, 0:tq] - lam * ovt[:, tq:]
        ot = ot * lax.rsqrt(jnp.mean(ot * ot, axis=0, keepdims=True) + EPS) * sg_ref[...] * (1.0 - lam_init)
        o_ref[...] = ot.T.astype(BF16)

    @pl.when(g % 2 == 0)
    def _():
        step(st0_ref, m0_ref, st1_ref, m1_ref)

    @pl.when(g % 2 == 1)
    def _():
        step(st1_ref, m1_ref, st0_ref, m0_ref)


def _attn(qk3, vt3, lam_vec, subln_g, lam_init):
    bsz, s, _ = qk3.shape
    tq = min(MAX_Q_TILE, SCORE_TILE_BYTES // (2 * s * 4))
    n_q = s // tq
    n_tiles = bsz * ATT_HEADS * n_q
    kcol = ATT_WIDTH // LANES
    w2 = lambda g: (0, 0)
    cur = lambda g: jnp.minimum(g, n_tiles - 1)
    prv = lambda g: jnp.maximum(g - 1, 0)
    bat = lambda t: t // (ATT_HEADS * n_q)
    head = lambda t: (t // n_q) % ATT_HEADS
    qi = lambda t: t % n_q
    return pl.pallas_call(
        functools.partial(_attn_kernel, lam_init=lam_init),
        out_shape=jax.ShapeDtypeStruct((bsz, s, ATT_WIDTH), BF16),
        grid=(n_tiles + 1,),
        in_specs=[
            pl.BlockSpec((None, tq, LANES), lambda g: (bat(cur(g)), qi(cur(g)), head(cur(g)))),
            pl.BlockSpec((None, s, LANES), lambda g: (bat(cur(g)), 0, kcol + head(cur(g)))),
            pl.BlockSpec((None, ATT_VDIM, s), lambda g: (bat(prv(g)), head(prv(g)), 0)),
            pl.BlockSpec((4, ATT_QKDIM), w2),
            pl.BlockSpec((ATT_VDIM, 1), w2),
        ],
        out_specs=pl.BlockSpec((None, tq, ATT_VDIM), lambda g: (bat(prv(g)), qi(prv(g)), head(prv(g)))),
        scratch_shapes=[pltpu.VMEM((s, 2 * tq), F32), pltpu.VMEM((s, 2 * tq), F32),
                        pltpu.VMEM((1, 2 * tq), F32), pltpu.VMEM((1, 2 * tq), F32)],
        compiler_params=_cparams("arbitrary"),
        name="attn",
    )(qk3, qk3, vt3, lam_vec, subln_g.reshape(ATT_VDIM, 1))


def _first_argmax(vals):
    best = vals[0]
    idx = jnp.zeros(best.shape, jnp.int32)
    for i in range(1, len(vals)):
        better = vals[i] > best
        idx = jnp.where(better, i, idx)
        best = jnp.where(better, vals[i], best)
    return idx, best


def _out_proj_kernel(x_ref, ya_ref, yb_ref, yc_ref, w_ref, mod_ref, g_ref, rw_ref, rb_ref,
                     xo_ref, hp_ref, meta_ref, cnt_ref):
    tm = TOKEN_TILE
    y = jnp.dot(ya_ref[...], w_ref[0:LRU_WIDTH, :], preferred_element_type=F32)
    y = y + jnp.dot(yb_ref[...], w_ref[LRU_WIDTH:LRU_WIDTH + CONV_WIDTH, :], preferred_element_type=F32)
    y = y + jnp.dot(yc_ref[...], w_ref[LRU_WIDTH + CONV_WIDTH:, :], preferred_element_type=F32)
    x = x_ref[...] + mod_ref[2:3, :] * y
    xo_ref[...] = x
    h = _rms(x, g_ref[...]) * (1.0 + mod_ref[4:5, :]) + mod_ref[3:4, :]
    hp_ref[:, 0:D_MODEL] = h

    logits = lax.dot_general(rw_ref[...], h, (((1,), (1,)), ((), ())), precision=lax.Precision.HIGHEST,
                             preferred_element_type=F32) + rb_ref[...]
    gl = [logits[g:g + 1, :] for g in range(N_GROUPS)]
    g_idx, g_max = _first_argmax(gl)
    denom = gl[0] * 0.0
    for g in range(N_GROUPS):
        denom = denom + jnp.exp(gl[g] - g_max)
    g_w = 1.0 / denom
    fl = []
    for e in range(EXPERTS_PER_GROUP):
        sel = logits[N_GROUPS + e:N_GROUPS + e + 1, :]
        for g in range(1, N_GROUPS):
            r0 = N_GROUPS + g * EXPERTS_PER_GROUP + e
            sel = jnp.where(g_idx == g, logits[r0:r0 + 1, :], sel)
        fl.append(sel)
    i1, v1 = _first_argmax(fl)
    neg = jnp.full(v1.shape, -jnp.inf, F32)
    i2, v2 = _first_argmax([jnp.where(i1 == e, neg, fl[e]) for e in range(EXPERTS_PER_GROUP)])
    t = jnp.exp(v2 - v1)
    p1 = g_w / (1.0 + t)
    p2 = g_w * t / (1.0 + t)
    first_lo = i1 < i2
    lo = jnp.where(first_lo, i1, i2)
    hi = jnp.where(first_lo, i2, i1)
    pair = jnp.where(lo == 0, 0, jnp.where(lo == 1, 3, 5)) + hi - lo - 1
    bucket = g_idx * PAIRS_PER_GROUP + pair
    p_lo = jnp.where(first_lo, p1, p2)
    p_hi = jnp.where(first_lo, p2, p1)
    wrows = jnp.concatenate([p_lo, p_hi, jnp.zeros((LANES - 2, tm), F32)], axis=0)
    hp_ref[:, D_MODEL:] = wrows.T

    @pl.when(pl.program_id(0) == 0)
    def _():
        cnt_ref[...] = jnp.zeros_like(cnt_ref)

    onehot = lax.broadcasted_iota(jnp.int32, (BUCKET_ROWS, tm), 0) == bucket
    before = lax.broadcasted_iota(jnp.int32, (tm, tm), 0) < lax.broadcasted_iota(jnp.int32, (tm, tm), 1)
    prefix = jnp.dot(jnp.where(onehot, 1.0, 0.0).astype(BF16), jnp.where(before, 1.0, 0.0).astype(BF16),
                     preferred_element_type=F32)
    base = cnt_ref[:, 0:1]
    rank = jnp.sum(jnp.where(onehot, prefix + base, 0.0), axis=0, keepdims=True)
    cnt_ref[...] = cnt_ref[...] + jnp.sum(jnp.where(onehot, 1.0, 0.0), axis=1, keepdims=True)
    meta_ref[...] = jnp.concatenate([bucket.astype(F32), rank, jnp.zeros((SUBLANES - 2, tm), F32)], axis=0)


def _out_proj(x, ya, yb, yc, w_out_bf, mod, g2, rw_t, rb, s):
    t = x.shape[0]
    tm = TOKEN_TILE
    per_seq = s // tm
    row = lambda i: (i, 0)
    w2 = lambda i: (0, 0)
    return pl.pallas_call(
        _out_proj_kernel,
        out_shape=(
            jax.ShapeDtypeStruct((t, D_MODEL), F32),
            jax.ShapeDtypeStruct((t, PAYLOAD_COLS), F32),
            jax.ShapeDtypeStruct((SUBLANES, t), F32),
            jax.ShapeDtypeStruct((BUCKET_ROWS, LANES), F32),
        ),
        grid=(t // tm,),
        in_specs=[
            pl.BlockSpec((tm, D_MODEL), row),
            pl.BlockSpec((tm, LRU_WIDTH), row),
            pl.BlockSpec((tm, CONV_WIDTH), row),
            pl.BlockSpec((tm, ATT_WIDTH), row),
            pl.BlockSpec((D_MODEL, D_MODEL), w2),
            pl.BlockSpec((None, 6, D_MODEL), lambda i: (i // per_seq, 0, 0)),
            pl.BlockSpec((1, D_MODEL), w2),
            pl.BlockSpec((ROUTER_ROWS, D_MODEL), w2),
            pl.BlockSpec((ROUTER_ROWS, 1), w2),
        ],
        out_specs=(
            pl.BlockSpec((tm, D_MODEL), row),
            pl.BlockSpec((tm, PAYLOAD_COLS), row),
            pl.BlockSpec((SUBLANES, tm), lambda i: (0, i)),
            pl.BlockSpec((BUCKET_ROWS, LANES), w2),
        ),
        compiler_params=_cparams("arbitrary"),
        name="out_proj",
    )(x, ya, yb, yc, w_out_bf, mod, g2, rw_t, rb)


def _route_tables(meta, cnt, t):
    tm = TOKEN_TILE
    n_tiles = t // tm + N_BUCKETS
    counts = cnt[:N_BUCKETS, 0].astype(jnp.int32)
    tiles_per = (counts + tm - 1) // tm
    tile_end = jnp.cumsum(tiles_per)
    offs = (tile_end - tiles_per) * tm
    n_used = tile_end[-1:]
    bucket = meta[0].astype(jnp.int32)
    rank = meta[1].astype(jnp.int32)
    ids = jnp.arange(N_BUCKETS, dtype=jnp.int32)
    dest = rank + jnp.sum(jnp.where(bucket[:, None] == ids[None, :], offs[None, :], 0), axis=1)
    tile = jnp.minimum(jnp.arange(n_tiles, dtype=jnp.int32), n_used - 1)
    tb = jnp.sum((tile[:, None] >= tile_end[None, :]).astype(jnp.int32), axis=1)
    pair_lo = jnp.array([0, 0, 0, 1, 1, 2], jnp.int32)
    pair_hi = jnp.array([1, 2, 3, 2, 3, 3], jnp.int32)
    onehot_pair = (tb % PAIRS_PER_GROUP)[:, None] == jnp.arange(PAIRS_PER_GROUP, dtype=jnp.int32)[None, :]
    e_lo = (tb // PAIRS_PER_GROUP) * EXPERTS_PER_GROUP + jnp.sum(jnp.where(onehot_pair, pair_lo[None, :], 0), axis=1)
    e_hi = (tb // PAIRS_PER_GROUP) * EXPERTS_PER_GROUP + jnp.sum(jnp.where(onehot_pair, pair_hi[None, :], 0), axis=1)
    return dest, jnp.stack([e_lo, e_hi], axis=1).reshape(-1), n_used, n_tiles


def _row_copy(src_ref, src_row, dst_ref, dst_row, sem):
    return pltpu.make_async_copy(src_ref.at[pl.ds(src_row, 1), :], dst_ref.at[pl.ds(dst_row, 1), :], sem)


def _dispatch_kernel(dest_ref, hp_ref, xs_in_ref, xs_ref, sem):
    del xs_in_ref
    tm = TOKEN_TILE
    base = pl.program_id(0) * tm

    def issue(r, carry):
        _row_copy(hp_ref, r, xs_ref, dest_ref[base + r], sem).start()
        return carry

    lax.fori_loop(0, tm, issue, 0, unroll=8)
    pltpu.make_async_copy(hp_ref, xs_ref.at[pl.ds(0, tm), :], sem).wait()


def _dispatch(dest, hp, n_rows):
    t = hp.shape[0]
    tm = TOKEN_TILE
    xs0 = jnp.zeros((n_rows, PAYLOAD_COLS), F32)
    return pl.pallas_call(
        _dispatch_kernel,
        out_shape=jax.ShapeDtypeStruct((n_rows, PAYLOAD_COLS), F32),
        grid_spec=pltpu.PrefetchScalarGridSpec(
            num_scalar_prefetch=1,
            grid=(t // tm,),
            in_specs=[
                pl.BlockSpec((tm, PAYLOAD_COLS), lambda i, dest: (i, 0)),
                pl.BlockSpec(memory_space=pl.ANY),
            ],
            out_specs=pl.BlockSpec(memory_space=pl.ANY),
            scratch_shapes=[pltpu.SemaphoreType.DMA],
        ),
        input_output_aliases={2: 0},
        compiler_params=_cparams("arbitrary"),
        name="dispatch",
    )(dest, hp, xs0)


def _experts_kernel(eid_ref, nused_ref, x_ref, w13a_ref, w2a_ref, w13b_ref, w2b_ref, o_ref):
    del eid_ref
    busy = pl.program_id(0) < nused_ref[0]

    @pl.when(jnp.logical_not(busy))
    def _():
        o_ref[...] = jnp.zeros_like(o_ref)

    @pl.when(busy)
    def _():
        x = x_ref[:, 0:D_MODEL].astype(BF16)

        def expert(w13_ref, w2_ref):
            h13 = jnp.dot(x, w13_ref[...], preferred_element_type=F32)
            a = h13[:, 0:D_EXPERT]
            he = (a * jax.nn.sigmoid(a)) * h13[:, D_EXPERT:]
            return jnp.dot(he.astype(BF16), w2_ref[...], preferred_element_type=F32)

        o_ref[...] = (x_ref[:, D_MODEL:D_MODEL + 1] * expert(w13a_ref, w2a_ref)
                      + x_ref[:, D_MODEL + 1:D_MODEL + 2] * expert(w13b_ref, w2b_ref))


def _experts(eids, n_used, xs, w13, w2):
    n_rows = xs.shape[0]
    tm = TOKEN_TILE
    rows = lambda i, eid, nu: (jnp.maximum(jnp.minimum(i, nu[0] - 1), 0), 0)
    wspec = lambda shape, slot: pl.BlockSpec((None,) + shape, lambda i, eid, nu: (eid[2 * i + slot], 0, 0))
    return pl.pallas_call(
        _experts_kernel,
        out_shape=jax.ShapeDtypeStruct((n_rows, D_MODEL), F32),
        grid_spec=pltpu.PrefetchScalarGridSpec(
            num_scalar_prefetch=2,
            grid=(n_rows // tm,),
            in_specs=[
                pl.BlockSpec((tm, PAYLOAD_COLS), rows),
                wspec((D_MODEL, 2 * D_EXPERT), 0),
                wspec((D_EXPERT, D_MODEL), 0),
                wspec((D_MODEL, 2 * D_EXPERT), 1),
                wspec((D_EXPERT, D_MODEL), 1),
            ],
            out_specs=pl.BlockSpec((tm, D_MODEL), lambda i, eid, nu: (i, 0)),
        ),
        compiler_params=_cparams("arbitrary"),
        name="experts",
    )(eids, n_used, xs, w13, w2, w13, w2)


def _combine_kernel(dest_ref, x_ref, ys_ref, mod_ref, g_ref, o_ref, ybuf_ref, sem, *, final):
    tm = TOKEN_TILE
    base = pl.program_id(0) * tm

    def issue(r, carry):
        _row_copy(ys_ref, dest_ref[base + r], ybuf_ref, r, sem).start()
        return carry

    lax.fori_loop(0, tm, issue, 0, unroll=8)
    pltpu.make_async_copy(ys_ref.at[pl.ds(0, tm), :], ybuf_ref, sem).wait()
    x = x_ref[...] + mod_ref[5:6, :] * ybuf_ref[...]
    o_ref[...] = _rms(x, g_ref[...]) if final else x


def _combine(dest, x, ys, mod, final_g, s, final):
    t = x.shape[0]
    tm = TOKEN_TILE
    per_seq = s // tm
    return pl.pallas_call(
        functools.partial(_combine_kernel, final=final),
        out_shape=jax.ShapeDtypeStruct((t, D_MODEL), F32),
        grid_spec=pltpu.PrefetchScalarGridSpec(
            num_scalar_prefetch=1,
            grid=(t // tm,),
            in_specs=[
                pl.BlockSpec((tm, D_MODEL), lambda i, dest: (i, 0)),
                pl.BlockSpec(memory_space=pl.ANY),
                pl.BlockSpec((None, 6, D_MODEL), lambda i, dest: (i // per_seq, 0, 0)),
                pl.BlockSpec((1, D_MODEL), lambda i, dest: (0, 0)),
            ],
            out_specs=pl.BlockSpec((tm, D_MODEL), lambda i, dest: (i, 0)),
            scratch_shapes=[pltpu.VMEM((tm, D_MODEL), F32), pltpu.SemaphoreType.DMA],
        ),
        compiler_params=_cparams("arbitrary"),
        name="combine",
    )(dest, x, ys, mod, final_g)


def _block_diag(w):
    h, dh, _ = w.shape
    eye = jnp.eye(h, dtype=w.dtype)
    return (eye[:, None, :, None] * w[:, :, None, :]).reshape(h * dh, h * dh)


def _prep_layer(l, p):
    row = lambda a: a.reshape(1, -1)
    rw = jnp.concatenate([p["router_w1"][l], p["router_w2"][l].reshape(D_MODEL, N_EXPERTS)], axis=1)
    rw = jnp.pad(rw, ((0, 0), (0, ROUTER_ROWS - rw.shape[1])))
    rb = jnp.concatenate([p["router_b1"][l], p["router_b2"][l].reshape(N_EXPERTS)])
    rb = jnp.pad(rb, (0, ROUTER_ROWS - rb.shape[0]))
    return dict(
        norm1_g=row(p["norm1_g"][l]),
        norm2_g=row(p["norm2_g"][l]),
        w_in=p["w_in"][l].astype(BF16),
        w_out=p["w_out"][l].astype(BF16),
        conv_w=p["conv_w"][l],
        conv_b=row(p["conv_b"][l]),
        wa_bd=jnp.stack([_block_diag(p["lru_wa"][l, d]) for d in range(2)]).astype(BF16),
        wx_bd=jnp.stack([_block_diag(p["lru_wx"][l, d]) for d in range(2)]).astype(BF16),
        ba=p["lru_ba"][l].reshape(2, 1, LRU_WIDTH),
        bx=p["lru_bx"][l].reshape(2, 1, LRU_WIDTH),
        lam=p["lru_lambda"][l].reshape(2, 1, LRU_WIDTH),
        dw_w=p["dw_w"][l],
        dw_b=row(p["dw_b"][l]),
        cln_g=row(p["cln_g"][l]),
        cln_b=row(p["cln_b"][l]),
        lam_vec=p["lam_vec"][l],
        subln_g=row(p["subln_g"][l]),
        rw_t=rw.T,
        rb=rb.reshape(ROUTER_ROWS, 1),
        w13=jnp.concatenate([p["moe_w1"][l], p["moe_w3"][l]], axis=-1).astype(BF16),
        w2=p["moe_w2"][l].astype(BF16),
    )


def _trunk(x, mods, layers, final_g):
    bsz, s, _ = x.shape
    t = bsz * s
    tables = _rope_tables(s)
    x = x.reshape(t, D_MODEL)
    for l, w in enumerate(layers):
        mod = mods[l]
        ab, qk, vt = _in_proj(x, mod, w["norm1_g"], w["w_in"], tables, s)
        ab3 = ab.reshape(bsz, s, AB_COLS)
        ya = _lru(ab3, w["conv_w"], w["conv_b"], w["wa_bd"], w["ba"], w["wx_bd"], w["bx"], w["lam"])
        yb = _convmod(ab3, w["dw_w"], w["dw_b"], w["cln_g"], w["cln_b"])
        lam_init = 0.8 - 0.6 * math.exp(-0.3 * l)
        yc = _attn(qk.reshape(bsz, s, QK_COLS), vt, w["lam_vec"], w["subln_g"], lam_init)
        x, hp, meta, cnt = _out_proj(x, ya.reshape(t, LRU_WIDTH), yb.reshape(t, CONV_WIDTH),
                                     yc.reshape(t, ATT_WIDTH), w["w_out"], mod, w["norm2_g"], w["rw_t"], w["rb"], s)
        dest, eids, n_used, n_tiles = _route_tables(meta, cnt, t)
        xs = _dispatch(dest, hp, n_tiles * TOKEN_TILE)
        ys = _experts(eids, n_used, xs, w["w13"], w["w2"])
        x = _combine(dest, x, ys, mod, final_g, s, final=(l == DEPTH - 1))
    return x.reshape(bsz, s, D_MODEL)


def kernel(x_prompt, x_sample, c_prompt, c_sample, norm1_g, norm2_g, final_g, ada_w, ada_b, w_in, w_out, conv_w, conv_b, lru_wa, lru_ba, lru_wx, lru_bx, lru_lambda, dw_w, dw_b, cln_g, cln_b, lam_vec, subln_g, router_w1, router_b1, router_w2, router_b2, moe_w1, moe_w3, moe_w2):
    p = dict(norm1_g=norm1_g, norm2_g=norm2_g, w_in=w_in, w_out=w_out, conv_w=conv_w, conv_b=conv_b,
             lru_wa=lru_wa, lru_ba=lru_ba, lru_wx=lru_wx, lru_bx=lru_bx, lru_lambda=lru_lambda,
             dw_w=dw_w, dw_b=dw_b, cln_g=cln_g, cln_b=cln_b, lam_vec=lam_vec, subln_g=subln_g,
             router_w1=router_w1, router_b1=router_b1, router_w2=router_w2, router_b2=router_b2,
             moe_w1=moe_w1, moe_w3=moe_w3, moe_w2=moe_w2)
    layers = [_prep_layer(l, p) for l in range(DEPTH)]
    nb = c_prompt.shape[0]
    mods = _ada_mod(jnp.concatenate([c_prompt, c_sample], axis=0), ada_w, ada_b)
    fg = final_g.reshape(1, D_MODEL)
    y_prompt = _trunk(x_prompt, mods[:, :nb], layers, fg)
    y_sample = _trunk(x_sample, mods[:, nb:], layers, fg)
    return (y_prompt, y_sample)
```

```python
import functools
import math

import jax
import jax.numpy as jnp
from jax import lax
from jax.experimental import pallas as pl
from jax.experimental.pallas import tpu as pltpu

F32 = jnp.float32
BF16 = jnp.bfloat16

D_MODEL = 1024
DEPTH = 2
LRU_WIDTH = 256
LRU_HEADS = 4
LRU_CONV = 4
LRU_C = 8.0
CONV_WIDTH = 256
CONV_KERNEL = 31
ATT_WIDTH = 512
ATT_HEADS = 4
ATT_VDIM = 128
ATT_QKDIM = 64
ROPE_DIM = 16
ROPE_THETA = 500000.0
AB_COLS = 2 * LRU_WIDTH + 2 * CONV_WIDTH
QK_COLS = 2 * ATT_WIDTH
IN_COLS = AB_COLS + QK_COLS + ATT_WIDTH
N_GROUPS = 4
EXPERTS_PER_GROUP = 4
N_EXPERTS = 16
D_EXPERT = 512
EPS = 1e-6

LANES = 128
SUBLANES = 8
ROUTER_ROWS = 24
PAIRS_PER_GROUP = 6
N_BUCKETS = N_GROUPS * PAIRS_PER_GROUP
BUCKET_ROWS = 32
PAYLOAD_COLS = D_MODEL + LANES
VMEM_LIMIT = 56 * 1024 * 1024

TOKEN_TILE = 512
SEQ_CHUNK = 512
SCORE_TILE_BYTES = 16 * 1024 * 1024
MAX_Q_TILE = 1024
ONES_ROWS = 16
KEY_CHUNK = 256
LRU_HALO = 8
CONV_HALO = 16


def _cparams(*sem):
    return pltpu.CompilerParams(dimension_semantics=sem, vmem_limit_bytes=VMEM_LIMIT)


def _rms(x, g):
    return x * lax.rsqrt(jnp.mean(x * x, axis=-1, keepdims=True) + EPS) * g


def _ada_kernel(c_ref, w_ref, b_ref, o_ref):
    c = c_ref[...]
    cs = (c * jax.nn.sigmoid(c)).astype(BF16)
    o_ref[...] = jnp.dot(cs, w_ref[...].astype(BF16), preferred_element_type=F32) + b_ref[...]


def _ada_mod(c, ada_w, ada_b):
    nb = c.shape[0]
    tn = 1536
    out = pl.pallas_call(
        _ada_kernel,
        out_shape=jax.ShapeDtypeStruct((DEPTH, nb, 6 * D_MODEL), F32),
        grid=(DEPTH, 6 * D_MODEL // tn),
        in_specs=[
            pl.BlockSpec((nb, D_MODEL), lambda l, j: (0, 0)),
            pl.BlockSpec((None, D_MODEL, tn), lambda l, j: (l, 0, j)),
            pl.BlockSpec((None, 1, tn), lambda l, j: (l, 0, j)),
        ],
        out_specs=pl.BlockSpec((None, nb, tn), lambda l, j: (l, 0, j)),
        compiler_params=_cparams("arbitrary", "arbitrary"),
        name="ada_mod",
    )(c, ada_w, ada_b.reshape(DEPTH, 1, 6 * D_MODEL))
    return out.reshape(DEPTH, nb, 6, D_MODEL)


def _in_proj_kernel(x_ref, mod_ref, g_ref, w_ref, cos_ref, sa_ref, sb_ref, ab_ref, qk_ref, vt_ref):
    x = x_ref[...]
    h = _rms(x, g_ref[...]) * (1.0 + mod_ref[1:2, :]) + mod_ref[0:1, :]
    hb = h.astype(BF16)
    ab_ref[...] = jnp.dot(hb, w_ref[:, 0:AB_COLS], preferred_element_type=F32)
    qk = jnp.dot(hb, w_ref[:, AB_COLS:AB_COLS + QK_COLS], preferred_element_type=F32)
    cos, sa, sb = cos_ref[...], sa_ref[...], sb_ref[...]
    for c in range(QK_COLS // LANES):
        blk = qk[:, c * LANES:(c + 1) * LANES]
        rot = (blk * cos + pltpu.roll(blk, ROPE_DIM // 2, 1) * sa
               + pltpu.roll(blk, LANES - ROPE_DIM // 2, 1) * sb)
        if c < ATT_WIDTH // LANES:
            rot = rot * (ATT_QKDIM ** -0.5)
        qk_ref[:, c * LANES:(c + 1) * LANES] = rot.astype(BF16)
    v = jnp.dot(hb, w_ref[:, AB_COLS + QK_COLS:], preferred_element_type=F32)
    vt_ref[...] = v.T.astype(BF16)


def _rope_tables(s):
    half = ROPE_DIM // 2
    inv = ROPE_THETA ** (-(jnp.arange(half, dtype=F32) * 2.0) / ROPE_DIM)
    ang = jnp.arange(s, dtype=jnp.int32).astype(F32)[:, None] * inv[None, :]
    cos, sin = jnp.cos(ang), jnp.sin(ang)
    rest = ATT_QKDIM - ROPE_DIM
    one = jnp.ones((s, rest), F32)
    zero = jnp.zeros((s, rest), F32)
    zh = jnp.zeros((s, half), F32)
    reps = LANES // ATT_QKDIM
    cos_t = jnp.tile(jnp.concatenate([cos, cos, one], axis=1), (1, reps))
    sa_t = jnp.tile(jnp.concatenate([zh, sin, zero], axis=1), (1, reps))
    sb_t = jnp.tile(jnp.concatenate([-sin, zh, zero], axis=1), (1, reps))
    return cos_t, sa_t, sb_t


def _in_proj(x, mod, g, w_in_bf, tables, s):
    t = x.shape[0]
    tm = TOKEN_TILE
    per_seq = s // tm
    row = lambda i: (i, 0)
    pos = lambda i: (i % per_seq, 0)
    return pl.pallas_call(
        _in_proj_kernel,
        out_shape=(
            jax.ShapeDtypeStruct((t, AB_COLS), F32),
            jax.ShapeDtypeStruct((t, QK_COLS), BF16),
            jax.ShapeDtypeStruct((t // s, ATT_WIDTH, s), BF16),
        ),
        grid=(t // tm,),
        in_specs=[
            pl.BlockSpec((tm, D_MODEL), row),
            pl.BlockSpec((None, 6, D_MODEL), lambda i: (i // per_seq, 0, 0)),
            pl.BlockSpec((1, D_MODEL), lambda i: (0, 0)),
            pl.BlockSpec((D_MODEL, IN_COLS), lambda i: (0, 0)),
            pl.BlockSpec((tm, LANES), pos),
            pl.BlockSpec((tm, LANES), pos),
            pl.BlockSpec((tm, LANES), pos),
        ],
        out_specs=(
            pl.BlockSpec((tm, AB_COLS), row),
            pl.BlockSpec((tm, QK_COLS), row),
            pl.BlockSpec((None, ATT_WIDTH, tm), lambda i: (i // per_seq, 0, i % per_seq)),
        ),
        compiler_params=_cparams("arbitrary"),
        name="in_proj",
    )(x, mod, g, w_in_bf, *tables)


def _neg_expm1(y):
    series = -y * (1.0 + y * (1.0 / 2.0) * (1.0 + y * (1.0 / 3.0) * (1.0 + y * (1.0 / 4.0) * (1.0 + y * (1.0 / 5.0)))))
    return jnp.where(y > -1.0 / 64.0, series, 1.0 - jnp.exp(y))


def _tile_scan(a, b, reverse):
    n = a.shape[0]
    a3 = a.reshape(n // SUBLANES, SUBLANES, a.shape[1])
    b3 = b.reshape(n // SUBLANES, SUBLANES, b.shape[1])
    row = lax.broadcasted_iota(jnp.int32, a3.shape, 1)
    for d in (1, 2, 4):
        shift = SUBLANES - d if reverse else d
        a_s = pltpu.roll(a3, shift, 1)
        b_s = pltpu.roll(b3, shift, 1)
        m = (row < SUBLANES - d) if reverse else (row >= d)
        b3 = jnp.where(m, b3 + a3 * b_s, b3)
        a3 = jnp.where(m, a3 * a_s, a3)
    return a3.reshape(a.shape), b3.reshape(b.shape)


def _lru_kernel(cur_ref, prev_ref, next_ref, ga_ref, cw_ref, cb_ref, wa_ref, ba_ref, wx_ref, bx_ref,
                lam_ref, o_ref, hf_ref, ext_ref, a_ref, b_ref, h_ref, carry_ref, *, n_chunks):
    tc = SEQ_CHUNK
    p = pl.program_id(1)
    j = pl.program_id(2)
    jx = j + p * (n_chunks - 1 - 2 * j)

    @pl.when(j == 0)
    def _():
        carry_ref[...] = jnp.zeros_like(carry_ref)

    zero_halo = jnp.zeros((LRU_HALO, LRU_WIDTH), F32)
    ext_ref[0:LRU_HALO, :] = jnp.where(jx == 0, zero_halo, prev_ref[...])
    ext_ref[LRU_HALO:LRU_HALO + tc, :] = cur_ref[...]
    ext_ref[LRU_HALO + tc:, :] = jnp.where(jx == n_chunks - 1, zero_halo, next_ref[...])
    xc = cb_ref[...]
    for k in range(LRU_CONV):
        off = LRU_HALO - 2 + k
        xc = xc + cw_ref[k:k + 1, :] * ext_ref[off:off + tc, :]
    xb = xc.astype(BF16)
    r = jax.nn.sigmoid(jnp.dot(xb, wa_ref[...], preferred_element_type=F32) + ba_ref[...])
    gate_i = jax.nn.sigmoid(jnp.dot(xb, wx_ref[...], preferred_element_type=F32) + bx_ref[...])
    log_a = (-LRU_C) * r * jax.nn.softplus(-lam_ref[...])
    a = jnp.exp(log_a)
    b = jnp.sqrt(_neg_expm1(2.0 * log_a)) * (gate_i * xc)

    n_tiles = tc // SUBLANES

    def run(reverse):
        a_s, b_s = _tile_scan(a, b, reverse)
        a_ref[...] = a_s
        b_ref[...] = b_s

        def body(i, hc):
            ti = (n_tiles - 1 - i) if reverse else i
            off = pl.multiple_of(ti * SUBLANES, SUBLANES)
            h = b_ref[pl.ds(off, SUBLANES), :] + a_ref[pl.ds(off, SUBLANES), :] * hc
            h_ref[pl.ds(off, SUBLANES), :] = h
            edge = h[0:1, :] if reverse else h[SUBLANES - 1:SUBLANES, :]
            return jnp.broadcast_to(edge, (SUBLANES, LRU_WIDTH))

        carry_ref[...] = lax.fori_loop(0, n_tiles, body, carry_ref[...], unroll=8)

    row0 = pl.multiple_of(jx * tc, tc)

    @pl.when(p == 0)
    def _():
        run(False)
        hf_ref[pl.ds(row0, tc), :] = h_ref[...]

    @pl.when(p == 1)
    def _():
        run(True)
        hsum = hf_ref[pl.ds(row0, tc), :] + h_ref[...]
        o_ref[...] = (jax.nn.gelu(ga_ref[...]) * hsum).astype(BF16)


def _lru(ab3, conv_w, conv_b, wa_bd, ba, wx_bd, bx, lam):
    bsz, s, _ = ab3.shape
    tc = SEQ_CHUNK
    n = s // tc
    hb = tc // LRU_HALO
    jx = lambda p, j: j + p * (n - 1 - 2 * j)
    jg = lambda p, j: n - 1 - p * j
    w2 = lambda b, p, j: (0, 0)
    wdir = lambda b, p, j: (p, 0, 0)
    return pl.pallas_call(
        functools.partial(_lru_kernel, n_chunks=n),
        out_shape=jax.ShapeDtypeStruct((bsz, s, LRU_WIDTH), BF16),
        grid=(bsz, 2, n),
        in_specs=[
            pl.BlockSpec((None, tc, LRU_WIDTH), lambda b, p, j: (b, jx(p, j), 0)),
            pl.BlockSpec((None, LRU_HALO, LRU_WIDTH),
                         lambda b, p, j: (b, jnp.maximum(jx(p, j) * hb - 1, 0), 0)),
            pl.BlockSpec((None, LRU_HALO, LRU_WIDTH),
                         lambda b, p, j: (b, jnp.minimum((jx(p, j) + 1) * hb, s // LRU_HALO - 1), 0)),
            pl.BlockSpec((None, tc, LRU_WIDTH), lambda b, p, j: (b, jg(p, j), 1)),
            pl.BlockSpec((LRU_CONV, LRU_WIDTH), w2),
            pl.BlockSpec((1, LRU_WIDTH), w2),
            pl.BlockSpec((None, LRU_WIDTH, LRU_WIDTH), wdir),
            pl.BlockSpec((None, 1, LRU_WIDTH), wdir),
            pl.BlockSpec((None, LRU_WIDTH, LRU_WIDTH), wdir),
            pl.BlockSpec((None, 1, LRU_WIDTH), wdir),
            pl.BlockSpec((None, 1, LRU_WIDTH), wdir),
        ],
        out_specs=pl.BlockSpec((None, tc, LRU_WIDTH), lambda b, p, j: (b, jg(p, j), 0)),
        scratch_shapes=[
            pltpu.VMEM((s, LRU_WIDTH), F32),
            pltpu.VMEM((tc + 2 * LRU_HALO, LRU_WIDTH), F32),
            pltpu.VMEM((tc, LRU_WIDTH), F32),
            pltpu.VMEM((tc, LRU_WIDTH), F32),
            pltpu.VMEM((tc, LRU_WIDTH), F32),
            pltpu.VMEM((SUBLANES, LRU_WIDTH), F32),
        ],
        compiler_params=_cparams("arbitrary", "arbitrary", "arbitrary"),
        name="lru",
    )(ab3, ab3, ab3, ab3, conv_w, conv_b, wa_bd, ba, wx_bd, bx, lam)


def _convmod_kernel(u_ref, g_ref, up_ref, gp_ref, un_ref, gn_ref, w_ref, b_ref, lg_ref, lb_ref,
                    o_ref, ext_ref, sh_ref, *, n_chunks):
    tc = SEQ_CHUNK
    j = pl.program_id(1)
    zero_halo = jnp.zeros((CONV_HALO, CONV_WIDTH), F32)
    glu = lambda u, g: u * jax.nn.sigmoid(g)
    ext_ref[0:CONV_HALO, :] = jnp.where(j == 0, zero_halo, glu(up_ref[...], gp_ref[...]))
    ext_ref[CONV_HALO:CONV_HALO + tc, :] = glu(u_ref[...], g_ref[...])
    ext_ref[CONV_HALO + tc:, :] = jnp.where(j == n_chunks - 1, zero_halo, glu(un_ref[...], gn_ref[...]))
    span = sh_ref.shape[1]
    for ph in range(1, SUBLANES):
        sh_ref[ph] = ext_ref[ph:ph + span, :]
    acc = b_ref[...]
    for k in range(CONV_KERNEL):
        off = CONV_HALO - CONV_KERNEL // 2 + k
        ph, base = off % SUBLANES, off - off % SUBLANES
        src = ext_ref[base:base + tc, :] if ph == 0 else sh_ref[ph, base:base + tc, :]
        acc = acc + w_ref[k:k + 1, :] * src
    mu = jnp.mean(acc, axis=-1, keepdims=True)
    cen = acc - mu
    var = jnp.mean(cen * cen, axis=-1, keepdims=True)
    z = cen * lax.rsqrt(var + EPS) * lg_ref[...] + lb_ref[...]
    o_ref[...] = (z * jax.nn.sigmoid(z)).astype(BF16)


def _convmod(ab3, dw_w, dw_b, cln_g, cln_b):
    bsz, s, _ = ab3.shape
    tc = SEQ_CHUNK
    n = s // tc
    hb = tc // CONV_HALO
    w2 = lambda b, j: (0, 0)
    ucol = 2 * LRU_WIDTH // CONV_WIDTH
    gcol = ucol + 1
    cur = lambda col: pl.BlockSpec((None, tc, CONV_WIDTH), lambda b, j: (b, j, col))
    prev = lambda col: pl.BlockSpec((None, CONV_HALO, CONV_WIDTH),
                                    lambda b, j: (b, jnp.maximum(j * hb - 1, 0), col))
    nxt = lambda col: pl.BlockSpec((None, CONV_HALO, CONV_WIDTH),
                                   lambda b, j: (b, jnp.minimum((j + 1) * hb, s // CONV_HALO - 1), col))
    return pl.pallas_call(
        functools.partial(_convmod_kernel, n_chunks=n),
        out_shape=jax.ShapeDtypeStruct((bsz, s, CONV_WIDTH), BF16),
        grid=(bsz, n),
        in_specs=[
            cur(ucol), cur(gcol), prev(ucol), prev(gcol), nxt(ucol), nxt(gcol),
            pl.BlockSpec((CONV_KERNEL, CONV_WIDTH), w2),
            pl.BlockSpec((1, CONV_WIDTH), w2),
            pl.BlockSpec((1, CONV_WIDTH), w2),
            pl.BlockSpec((1, CONV_WIDTH), w2),
        ],
        out_specs=pl.BlockSpec((None, tc, CONV_WIDTH), lambda b, j: (b, j, 0)),
        scratch_shapes=[pltpu.VMEM((tc + 2 * CONV_HALO, CONV_WIDTH), F32),
                        pltpu.VMEM((SUBLANES, tc + 2 * CONV_HALO - SUBLANES, CONV_WIDTH), F32)],
        compiler_params=_cparams("arbitrary", "arbitrary"),
        name="convmod",
    )(ab3, ab3, ab3, ab3, ab3, ab3, dw_w, dw_b, cln_g, cln_b)


def _attn_kernel(q_ref, k_ref, vt_ref, lv_ref, sg_ref, o_ref, st0_ref, st1_ref, m0_ref, m1_ref, *, lam_init):
    tq = q_ref.shape[0]
    g = pl.program_id(0)

    @pl.when(g == 0)
    def _():
        st1_ref[...] = jnp.zeros_like(st1_ref)
        m1_ref[...] = jnp.zeros_like(m1_ref)

    def step(st_new_ref, m_new_ref, st_old_ref, m_old_ref):
        q = q_ref[...]
        lane = lax.broadcasted_iota(jnp.int32, q.shape, 1)
        zero = jnp.zeros_like(q)
        qq = jnp.concatenate([jnp.where(lane < ATT_QKDIM, q, zero), jnp.where(lane >= ATT_QKDIM, q, zero)], axis=0)
        kc = KEY_CHUNK
        n_chunks = k_ref.shape[0] // kc
        ones = jnp.ones((ONES_ROWS, kc), BF16)
        m_old = m_old_ref[...]
        chunk = lambda c: slice(c * kc, (c + 1) * kc)

        def scores(c, m_new):
            st_c = lax.dot_general(k_ref[chunk(c), :], qq, (((1,), (1,)), ((), ())), preferred_element_type=F32)
            st_new_ref[chunk(c), :] = st_c
            m_c = jnp.max(st_c, axis=0, keepdims=True)
            return m_c if m_new is None else jnp.maximum(m_new, m_c)

        m_new = scores(0, None)
        ovt = None
        for c in range(n_chunks):
            e_c = jnp.exp(st_old_ref[chunk(c), :] - m_old).astype(BF16)
            if c + 1 < n_chunks:
                m_new = scores(c + 1, m_new)
            vt_c = jnp.concatenate([vt_ref[:, chunk(c)], ones], axis=0)
            o_c = jnp.dot(vt_c, e_c, preferred_element_type=F32)
            ovt = o_c if ovt is None else ovt + o_c
        m_new_ref[...] = m_new
        ovt = ovt[0:ATT_VDIM, :] / ovt[ATT_VDIM:ATT_VDIM + 1, :]
        lv = lv_ref[...]
        lam = (jnp.exp(jnp.sum(lv[0:1, :] * lv[1:2, :], axis=-1, keepdims=True))
               - jnp.exp(jnp.sum(lv[2:3, :] * lv[3:4, :], axis=-1, keepdims=True)) + lam_init)
        ot = ovt[:, 0:tq] - lam * ovt[:, tq:]
        ot = ot * lax.rsqrt(jnp.mean(ot * ot, axis=0, keepdims=True) + EPS) * sg_ref[...] * (1.0 - lam_init)
        o_ref[...] = ot.T.astype(BF16)

    @pl.when(g % 2 == 0)
    def _():
        step(st0_ref, m0_ref, st1_ref, m1_ref)

    @pl.when(g % 2 == 1)
    def _():
        step(st1_ref, m1_ref, st0_ref, m0_ref)


def _attn(qk3, vt3, lam_vec, subln_g, lam_init):
    bsz, s, _ = qk3.shape
    tq = min(MAX_Q_TILE, SCORE_TILE_BYTES // (2 * s * 4))
    n_q = s // tq
    n_tiles = bsz * ATT_HEADS * n_q
    kcol = ATT_WIDTH // LANES
    w2 = lambda g: (0, 0)
    cur = lambda g: jnp.minimum(g, n_tiles - 1)
    prv = lambda g: jnp.maximum(g - 1, 0)
    bat = lambda t: t // (ATT_HEADS * n_q)
    head = lambda t: (t // n_q) % ATT_HEADS
    qi = lambda t: t % n_q
    return pl.pallas_call(
        functools.partial(_attn_kernel, lam_init=lam_init),
        out_shape=jax.ShapeDtypeStruct((bsz, s, ATT_WIDTH), BF16),
        grid=(n_tiles + 1,),
        in_specs=[
            pl.BlockSpec((None, tq, LANES), lambda g: (bat(cur(g)), qi(cur(g)), head(cur(g)))),
            pl.BlockSpec((None, s, LANES), lambda g: (bat(cur(g)), 0, kcol + head(cur(g)))),
            pl.BlockSpec((None, ATT_VDIM, s), lambda g: (bat(prv(g)), head(prv(g)), 0)),
            pl.BlockSpec((4, ATT_QKDIM), w2),
            pl.BlockSpec((ATT_VDIM, 1), w2),
        ],
        out_specs=pl.BlockSpec((None, tq, ATT_VDIM), lambda g: (bat(prv(g)), qi(prv(g)), head(prv(g)))),
        scratch_shapes=[pltpu.VMEM((s, 2 * tq), F32), pltpu.VMEM((s, 2 * tq), F32),
                        pltpu.VMEM((1, 2 * tq), F32), pltpu.VMEM((1, 2 * tq), F32)],
        compiler_params=_cparams("arbitrary"),
        name="attn",
    )(qk3, qk3, vt3, lam_vec, subln_g.reshape(ATT_VDIM, 1))


def _first_argmax(vals):
    best = vals[0]
    idx = jnp.zeros(best.shape, jnp.int32)
    for i in range(1, len(vals)):
        better = vals[i] > best
        idx = jnp.where(better, i, idx)
        best = jnp.where(better, vals[i], best)
    return idx, best


def _out_proj_kernel(x_ref, ya_ref, yb_ref, yc_ref, w_ref, mod_ref, g_ref, rw_ref, rb_ref,
                     xo_ref, hp_ref, meta_ref, cnt_ref):
    tm = TOKEN_TILE
    y = jnp.dot(ya_ref[...], w_ref[0:LRU_WIDTH, :], preferred_element_type=F32)
    y = y + jnp.dot(yb_ref[...], w_ref[LRU_WIDTH:LRU_WIDTH + CONV_WIDTH, :], preferred_element_type=F32)
    y = y + jnp.dot(yc_ref[...], w_ref[LRU_WIDTH + CONV_WIDTH:, :], preferred_element_type=F32)
    x = x_ref[...] + mod_ref[2:3, :] * y
    xo_ref[...] = x
    h = _rms(x, g_ref[...]) * (1.0 + mod_ref[4:5, :]) + mod_ref[3:4, :]
    hp_ref[:, 0:D_MODEL] = h

    logits = lax.dot_general(rw_ref[...], h, (((1,), (1,)), ((), ())), precision=lax.Precision.HIGHEST,
                             preferred_element_type=F32) + rb_ref[...]
    gl = [logits[g:g + 1, :] for g in range(N_GROUPS)]
    g_idx, g_max = _first_argmax(gl)
    denom = gl[0] * 0.0
    for g in range(N_GROUPS):
        denom = denom + jnp.exp(gl[g] - g_max)
    g_w = 1.0 / denom
    fl = []
    for e in range(EXPERTS_PER_GROUP):
        sel = logits[N_GROUPS + e:N_GROUPS + e + 1, :]
        for g in range(1, N_GROUPS):
            r0 = N_GROUPS + g * EXPERTS_PER_GROUP + e
            sel = jnp.where(g_idx == g, logits[r0:r0 + 1, :], sel)
        fl.append(sel)
    i1, v1 = _first_argmax(fl)
    neg = jnp.full(v1.shape, -jnp.inf, F32)
    i2, v2 = _first_argmax([jnp.where(i1 == e, neg, fl[e]) for e in range(EXPERTS_PER_GROUP)])
    t = jnp.exp(v2 - v1)
    p1 = g_w / (1.0 + t)
    p2 = g_w * t / (1.0 + t)
    first_lo = i1 < i2
    lo = jnp.where(first_lo, i1, i2)
    hi = jnp.where(first_lo, i2, i1)
    pair = jnp.where(lo == 0, 0, jnp.where(lo == 1, 3, 5)) + hi - lo - 1
    bucket = g_idx * PAIRS_PER_GROUP + pair
    p_lo = jnp.where(first_lo, p1, p2)
    p_hi = jnp.where(first_lo, p2, p1)
    wrows = jnp.concatenate([p_lo, p_hi, jnp.zeros((LANES - 2, tm), F32)], axis=0)
    hp_ref[:, D_MODEL:] = wrows.T

    @pl.when(pl.program_id(0) == 0)
    def _():
        cnt_ref[...] = jnp.zeros_like(cnt_ref)

    onehot = lax.broadcasted_iota(jnp.int32, (BUCKET_ROWS, tm), 0) == bucket
    before = lax.broadcasted_iota(jnp.int32, (tm, tm), 0) < lax.broadcasted_iota(jnp.int32, (tm, tm), 1)
    prefix = jnp.dot(jnp.where(onehot, 1.0, 0.0).astype(BF16), jnp.where(before, 1.0, 0.0).astype(BF16),
                     preferred_element_type=F32)
    base = cnt_ref[:, 0:1]
    rank = jnp.sum(jnp.where(onehot, prefix + base, 0.0), axis=0, keepdims=True)
    cnt_ref[...] = cnt_ref[...] + jnp.sum(jnp.where(onehot, 1.0, 0.0), axis=1, keepdims=True)
    meta_ref[...] = jnp.concatenate([bucket.astype(F32), rank, jnp.zeros((SUBLANES - 2, tm), F32)], axis=0)


def _out_proj(x, ya, yb, yc, w_out_bf, mod, g2, rw_t, rb, s):
    t = x.shape[0]
    tm = TOKEN_TILE
    per_seq = s // tm
    row = lambda i: (i, 0)
    w2 = lambda i: (0, 0)
    return pl.pallas_call(
        _out_proj_kernel,
        out_shape=(
            jax.ShapeDtypeStruct((t, D_MODEL), F32),
            jax.ShapeDtypeStruct((t, PAYLOAD_COLS), F32),
            jax.ShapeDtypeStruct((SUBLANES, t), F32),
            jax.ShapeDtypeStruct((BUCKET_ROWS, LANES), F32),
        ),
        grid=(t // tm,),
        in_specs=[
            pl.BlockSpec((tm, D_MODEL), row),
            pl.BlockSpec((tm, LRU_WIDTH), row),
            pl.BlockSpec((tm, CONV_WIDTH), row),
            pl.BlockSpec((tm, ATT_WIDTH), row),
            pl.BlockSpec((D_MODEL, D_MODEL), w2),
            pl.BlockSpec((None, 6, D_MODEL), lambda i: (i // per_seq, 0, 0)),
            pl.BlockSpec((1, D_MODEL), w2),
            pl.BlockSpec((ROUTER_ROWS, D_MODEL), w2),
            pl.BlockSpec((ROUTER_ROWS, 1), w2),
        ],
        out_specs=(
            pl.BlockSpec((tm, D_MODEL), row),
            pl.BlockSpec((tm, PAYLOAD_COLS), row),
            pl.BlockSpec((SUBLANES, tm), lambda i: (0, i)),
            pl.BlockSpec((BUCKET_ROWS, LANES), w2),
        ),
        compiler_params=_cparams("arbitrary"),
        name="out_proj",
    )(x, ya, yb, yc, w_out_bf, mod, g2, rw_t, rb)


def _route_tables(meta, cnt, t):
    tm = TOKEN_TILE
    n_tiles = t // tm + N_BUCKETS
    counts = cnt[:N_BUCKETS, 0].astype(jnp.int32)
    tiles_per = (counts + tm - 1) // tm
    tile_end = jnp.cumsum(tiles_per)
    offs = (tile_end - tiles_per) * tm
    n_used = tile_end[-1:]
    bucket = meta[0].astype(jnp.int32)
    rank = meta[1].astype(jnp.int32)
    ids = jnp.arange(N_BUCKETS, dtype=jnp.int32)
    dest = rank + jnp.sum(jnp.where(bucket[:, None] == ids[None, :], offs[None, :], 0), axis=1)
    tile = jnp.minimum(jnp.arange(n_tiles, dtype=jnp.int32), n_used - 1)
    tb = jnp.sum((tile[:, None] >= tile_end[None, :]).astype(jnp.int32), axis=1)
    pair_lo = jnp.array([0, 0, 0, 1, 1, 2], jnp.int32)
    pair_hi = jnp.array([1, 2, 3, 2, 3, 3], jnp.int32)
    onehot_pair = (tb % PAIRS_PER_GROUP)[:, None] == jnp.arange(PAIRS_PER_GROUP, dtype=jnp.int32)[None, :]
    e_lo = (tb // PAIRS_PER_GROUP) * EXPERTS_PER_GROUP + jnp.sum(jnp.where(onehot_pair, pair_lo[None, :], 0), axis=1)
    e_hi = (tb // PAIRS_PER_GROUP) * EXPERTS_PER_GROUP + jnp.sum(jnp.where(onehot_pair, pair_hi[None, :], 0), axis=1)
    return dest, jnp.stack([e_lo, e_hi], axis=1).reshape(-1), n_used, n_tiles


def _row_copy(src_ref, src_row, dst_ref, dst_row, sem):
    return pltpu.make_async_copy(src_ref.at[pl.ds(src_row, 1), :], dst_ref.at[pl.ds(dst_row, 1), :], sem)


def _dispatch_kernel(dest_ref, hp_ref, xs_in_ref, xs_ref, sem):
    del xs_in_ref
    tm = TOKEN_TILE
    base = pl.program_id(0) * tm

    def issue(r, carry):
        _row_copy(hp_ref, r, xs_ref, dest_ref[base + r], sem).start()
        return carry

    lax.fori_loop(0, tm, issue, 0, unroll=8)
    pltpu.make_async_copy(hp_ref, xs_ref.at[pl.ds(0, tm), :], sem).wait()


def _dispatch(dest, hp, n_rows):
    t = hp.shape[0]
    tm = TOKEN_TILE
    xs0 = jnp.zeros((n_rows, PAYLOAD_COLS), F32)
    return pl.pallas_call(
        _dispatch_kernel,
        out_shape=jax.ShapeDtypeStruct((n_rows, PAYLOAD_COLS), F32),
        grid_spec=pltpu.PrefetchScalarGridSpec(
            num_scalar_prefetch=1,
            grid=(t // tm,),
            in_specs=[
                pl.BlockSpec((tm, PAYLOAD_COLS), lambda i, dest: (i, 0)),
                pl.BlockSpec(memory_space=pl.ANY),
            ],
            out_specs=pl.BlockSpec(memory_space=pl.ANY),
            scratch_shapes=[pltpu.SemaphoreType.DMA],
        ),
        input_output_aliases={2: 0},
        compiler_params=_cparams("arbitrary"),
        name="dispatch",
    )(dest, hp, xs0)


def _experts_kernel(eid_ref, nused_ref, x_ref, w13a_ref, w2a_ref, w13b_ref, w2b_ref, o_ref):
    del eid_ref
    busy = pl.program_id(0) < nused_ref[0]

    @pl.when(jnp.logical_not(busy))
    def _():
        o_ref[...] = jnp.zeros_like(o_ref)

    @pl.when(busy)
    def _():
        x = x_ref[:, 0:D_MODEL].astype(BF16)

        def expert(w13_ref, w2_ref):
            h13 = jnp.dot(x, w13_ref[...], preferred_element_type=F32)
            a = h13[:, 0:D_EXPERT]
            he = (a * jax.nn.sigmoid(a)) * h13[:, D_EXPERT:]
            return jnp.dot(he.astype(BF16), w2_ref[...], preferred_element_type=F32)

        o_ref[...] = (x_ref[:, D_MODEL:D_MODEL + 1] * expert(w13a_ref, w2a_ref)
                      + x_ref[:, D_MODEL + 1:D_MODEL + 2] * expert(w13b_ref, w2b_ref))


def _experts(eids, n_used, xs, w13, w2):
    n_rows = xs.shape[0]
    tm = TOKEN_TILE
    rows = lambda i, eid, nu: (jnp.maximum(jnp.minimum(i, nu[0] - 1), 0), 0)
    wspec = lambda shape, slot: pl.BlockSpec((None,) + shape, lambda i, eid, nu: (eid[2 * i + slot], 0, 0))
    return pl.pallas_call(
        _experts_kernel,
        out_shape=jax.ShapeDtypeStruct((n_rows, D_MODEL), F32),
        grid_spec=pltpu.PrefetchScalarGridSpec(
            num_scalar_prefetch=2,
            grid=(n_rows // tm,),
            in_specs=[
                pl.BlockSpec((tm, PAYLOAD_COLS), rows),
                wspec((D_MODEL, 2 * D_EXPERT), 0),
                wspec((D_EXPERT, D_MODEL), 0),
                wspec((D_MODEL, 2 * D_EXPERT), 1),
                wspec((D_EXPERT, D_MODEL), 1),
            ],
            out_specs=pl.BlockSpec((tm, D_MODEL), lambda i, eid, nu: (i, 0)),
        ),
        compiler_params=_cparams("arbitrary"),
        name="experts",
    )(eids, n_used, xs, w13, w2, w13, w2)


def _combine_kernel(dest_ref, x_ref, ys_ref, mod_ref, g_ref, o_ref, ybuf_ref, sem, *, final):
    tm = TOKEN_TILE
    base = pl.program_id(0) * tm

    def issue(r, carry):
        _row_copy(ys_ref, dest_ref[base + r], ybuf_ref, r, sem).start()
        return carry

    lax.fori_loop(0, tm, issue, 0, unroll=8)
    pltpu.make_async_copy(ys_ref.at[pl.ds(0, tm), :], ybuf_ref, sem).wait()
    x = x_ref[...] + mod_ref[5:6, :] * ybuf_ref[...]
    o_ref[...] = _rms(x, g_ref[...]) if final else x


def _combine(dest, x, ys, mod, final_g, s, final):
    t = x.shape[0]
    tm = TOKEN_TILE
    per_seq = s // tm
    return pl.pallas_call(
        functools.partial(_combine_kernel, final=final),
        out_shape=jax.ShapeDtypeStruct((t, D_MODEL), F32),
        grid_spec=pltpu.PrefetchScalarGridSpec(
            num_scalar_prefetch=1,
            grid=(t // tm,),
            in_specs=[
                pl.BlockSpec((tm, D_MODEL), lambda i, dest: (i, 0)),
                pl.BlockSpec(memory_space=pl.ANY),
                pl.BlockSpec((None, 6, D_MODEL), lambda i, dest: (i // per_seq, 0, 0)),
                pl.BlockSpec((1, D_MODEL), lambda i, dest: (0, 0)),
            ],
            out_specs=pl.BlockSpec((tm, D_MODEL), lambda i, dest: (i, 0)),
            scratch_shapes=[pltpu.VMEM((tm, D_MODEL), F32), pltpu.SemaphoreType.DMA],
        ),
        compiler_params=_cparams("arbitrary"),
        name="combine",
    )(dest, x, ys, mod, final_g)


def _block_diag(w):
    h, dh, _ = w.shape
    eye = jnp.eye(h, dtype=w.dtype)
    return (eye[:, None, :, None] * w[:, :, None, :]).reshape(h * dh, h * dh)


def _prep_layer(l, p):
    row = lambda a: a.reshape(1, -1)
    rw = jnp.concatenate([p["router_w1"][l], p["router_w2"][l].reshape(D_MODEL, N_EXPERTS)], axis=1)
    rw = jnp.pad(rw, ((0, 0), (0, ROUTER_ROWS - rw.shape[1])))
    rb = jnp.concatenate([p["router_b1"][l], p["router_b2"][l].reshape(N_EXPERTS)])
    rb = jnp.pad(rb, (0, ROUTER_ROWS - rb.shape[0]))
    return dict(
        norm1_g=row(p["norm1_g"][l]),
        norm2_g=row(p["norm2_g"][l]),
        w_in=p["w_in"][l].astype(BF16),
        w_out=p["w_out"][l].astype(BF16),
        conv_w=p["conv_w"][l],
        conv_b=row(p["conv_b"][l]),
        wa_bd=jnp.stack([_block_diag(p["lru_wa"][l, d]) for d in range(2)]).astype(BF16),
        wx_bd=jnp.stack([_block_diag(p["lru_wx"][l, d]) for d in range(2)]).astype(BF16),
        ba=p["lru_ba"][l].reshape(2, 1, LRU_WIDTH),
        bx=p["lru_bx"][l].reshape(2, 1, LRU_WIDTH),
        lam=p["lru_lambda"][l].reshape(2, 1, LRU_WIDTH),
        dw_w=p["dw_w"][l],
        dw_b=row(p["dw_b"][l]),
        cln_g=row(p["cln_g"][l]),
        cln_b=row(p["cln_b"][l]),
        lam_vec=p["lam_vec"][l],
        subln_g=row(p["subln_g"][l]),
        rw_t=rw.T,
        rb=rb.reshape(ROUTER_ROWS, 1),
        w13=jnp.concatenate([p["moe_w1"][l], p["moe_w3"][l]], axis=-1).astype(BF16),
        w2=p["moe_w2"][l].astype(BF16),
    )


def _trunk(x, mods, layers, final_g):
    bsz, s, _ = x.shape
    t = bsz * s
    tables = _rope_tables(s)
    x = x.reshape(t, D_MODEL)
    for l, w in enumerate(layers):
        mod = mods[l]
        ab, qk, vt = _in_proj(x, mod, w["norm1_g"], w["w_in"], tables, s)
        ab3 = ab.reshape(bsz, s, AB_COLS)
        ya = _lru(ab3, w["conv_w"], w["conv_b"], w["wa_bd"], w["ba"], w["wx_bd"], w["bx"], w["lam"])
        yb = _convmod(ab3, w["dw_w"], w["dw_b"], w["cln_g"], w["cln_b"])
        lam_init = 0.8 - 0.6 * math.exp(-0.3 * l)
        yc = _attn(qk.reshape(bsz, s, QK_COLS), vt, w["lam_vec"], w["subln_g"], lam_init)
        x, hp, meta, cnt = _out_proj(x, ya.reshape(t, LRU_WIDTH), yb.reshape(t, CONV_WIDTH),
                                     yc.reshape(t, ATT_WIDTH), w["w_out"], mod, w["norm2_g"], w["rw_t"], w["rb"], s)
        dest, eids, n_used, n_tiles = _route_tables(meta, cnt, t)
        xs = _dispatch(dest, hp, n_tiles * TOKEN_TILE)
        ys = _experts(eids, n_used, xs, w["w13"], w["w2"])
        x = _combine(dest, x, ys, mod, final_g, s, final=(l == DEPTH - 1))
    return x.reshape(bsz, s, D_MODEL)


def kernel(x_prompt, x_sample, c_prompt, c_sample, norm1_g, norm2_g, final_g, ada_w, ada_b, w_in, w_out, conv_w, conv_b, lru_wa, lru_ba, lru_wx, lru_bx, lru_lambda, dw_w, dw_b, cln_g, cln_b, lam_vec, subln_g, router_w1, router_b1, router_w2, router_b2, moe_w1, moe_w3, moe_w2):
    p = dict(norm1_g=norm1_g, norm2_g=norm2_g, w_in=w_in, w_out=w_out, conv_w=conv_w, conv_b=conv_b,
             lru_wa=lru_wa, lru_ba=lru_ba, lru_wx=lru_wx, lru_bx=lru_bx, lru_lambda=lru_lambda,
             dw_w=dw_w, dw_b=dw_b, cln_g=cln_g, cln_b=cln_b, lam_vec=lam_vec, subln_g=subln_g,
             router_w1=router_w1, router_b1=router_b1, router_w2=router_w2, router_b2=router_b2,
             moe_w1=moe_w1, moe_w3=moe_w3, moe_w2=moe_w2)
    layers = [_prep_layer(l, p) for l in range(DEPTH)]
    nb = c_prompt.shape[0]
    mods = _ada_mod(jnp.concatenate([c_prompt, c_sample], axis=0), ada_w, ada_b)
    fg = final_g.reshape(1, D_MODEL)
    y_prompt = _trunk(x_prompt, mods[:, :nb], layers, fg)
    y_sample = _trunk(x_sample, mods[:, nb:], layers, fg)
    return (y_prompt, y_sample)
```

```python
import functools
import math

import jax
import jax.numpy as jnp
from jax import lax
from jax.experimental import pallas as pl
from jax.experimental.pallas import tpu as pltpu

F32 = jnp.float32
BF16 = jnp.bfloat16

D_MODEL = 1024
DEPTH = 2
LRU_WIDTH = 256
LRU_HEADS = 4
LRU_CONV = 4
LRU_C = 8.0
CONV_WIDTH = 256
CONV_KERNEL = 31
ATT_WIDTH = 512
ATT_HEADS = 4
ATT_VDIM = 128
ATT_QKDIM = 64
ROPE_DIM = 16
ROPE_THETA = 500000.0
AB_COLS = 2 * LRU_WIDTH + 2 * CONV_WIDTH
QK_COLS = 2 * ATT_WIDTH
IN_COLS = AB_COLS + QK_COLS + ATT_WIDTH
N_GROUPS = 4
EXPERTS_PER_GROUP = 4
N_EXPERTS = 16
D_EXPERT = 512
EPS = 1e-6

LANES = 128
SUBLANES = 8
ROUTER_ROWS = 24
PAIRS_PER_GROUP = 6
N_BUCKETS = N_GROUPS * PAIRS_PER_GROUP
BUCKET_ROWS = 32
PAYLOAD_COLS = D_MODEL + LANES
VMEM_LIMIT = 56 * 1024 * 1024

TOKEN_TILE = 512
SEQ_CHUNK = 512
SCORE_TILE_BYTES = 16 * 1024 * 1024
MAX_Q_TILE = 512
ONES_ROWS = 16
KEY_CHUNK = 256
LRU_HALO = 8
CONV_HALO = 16


def _cparams(*sem):
    return pltpu.CompilerParams(dimension_semantics=sem, vmem_limit_bytes=VMEM_LIMIT)


def _rms(x, g):
    return x * lax.rsqrt(jnp.mean(x * x, axis=-1, keepdims=True) + EPS) * g


def _ada_kernel(c_ref, w_ref, b_ref, o_ref):
    c = c_ref[...]
    cs = (c * jax.nn.sigmoid(c)).astype(BF16)
    o_ref[...] = jnp.dot(cs, w_ref[...].astype(BF16), preferred_element_type=F32) + b_ref[...]


def _ada_mod(c, ada_w, ada_b):
    nb = c.shape[0]
    tn = 1536
    out = pl.pallas_call(
        _ada_kernel,
        out_shape=jax.ShapeDtypeStruct((DEPTH, nb, 6 * D_MODEL), F32),
        grid=(DEPTH, 6 * D_MODEL // tn),
        in_specs=[
            pl.BlockSpec((nb, D_MODEL), lambda l, j: (0, 0)),
            pl.BlockSpec((None, D_MODEL, tn), lambda l, j: (l, 0, j)),
            pl.BlockSpec((None, 1, tn), lambda l, j: (l, 0, j)),
        ],
        out_specs=pl.BlockSpec((None, nb, tn), lambda l, j: (l, 0, j)),
        compiler_params=_cparams("arbitrary", "arbitrary"),
        name="ada_mod",
    )(c, ada_w, ada_b.reshape(DEPTH, 1, 6 * D_MODEL))
    return out.reshape(DEPTH, nb, 6, D_MODEL)


def _in_proj_kernel(x_ref, mod_ref, g_ref, w_ref, cos_ref, sa_ref, sb_ref, ab_ref, qk_ref, vt_ref):
    x = x_ref[...]
    h = _rms(x, g_ref[...]) * (1.0 + mod_ref[1:2, :]) + mod_ref[0:1, :]
    hb = h.astype(BF16)
    ab_ref[...] = jnp.dot(hb, w_ref[:, 0:AB_COLS], preferred_element_type=F32)
    qk = jnp.dot(hb, w_ref[:, AB_COLS:AB_COLS + QK_COLS], preferred_element_type=F32)
    cos, sa, sb = cos_ref[...], sa_ref[...], sb_ref[...]
    for c in range(QK_COLS // LANES):
        blk = qk[:, c * LANES:(c + 1) * LANES]
        rot = (blk * cos + pltpu.roll(blk, ROPE_DIM // 2, 1) * sa
               + pltpu.roll(blk, LANES - ROPE_DIM // 2, 1) * sb)
        if c < ATT_WIDTH // LANES:
            rot = rot * (ATT_QKDIM ** -0.5)
        qk_ref[:, c * LANES:(c + 1) * LANES] = rot.astype(BF16)
    v = jnp.dot(hb, w_ref[:, AB_COLS + QK_COLS:], preferred_element_type=F32)
    vt_ref[...] = v.T.astype(BF16)


def _rope_tables(s):
    half = ROPE_DIM // 2
    inv = ROPE_THETA ** (-(jnp.arange(half, dtype=F32) * 2.0) / ROPE_DIM)
    ang = jnp.arange(s, dtype=jnp.int32).astype(F32)[:, None] * inv[None, :]
    cos, sin = jnp.cos(ang), jnp.sin(ang)
    rest = ATT_QKDIM - ROPE_DIM
    one = jnp.ones((s, rest), F32)
    zero = jnp.zeros((s, rest), F32)
    zh = jnp.zeros((s, half), F32)
    reps = LANES // ATT_QKDIM
    cos_t = jnp.tile(jnp.concatenate([cos, cos, one], axis=1), (1, reps))
    sa_t = jnp.tile(jnp.concatenate([zh, sin, zero], axis=1), (1, reps))
    sb_t = jnp.tile(jnp.concatenate([-sin, zh, zero], axis=1), (1, reps))
    return cos_t, sa_t, sb_t


def _in_proj(x, mod, g, w_in_bf, tables, s):
    t = x.shape[0]
    tm = TOKEN_TILE
    per_seq = s // tm
    row = lambda i: (i, 0)
    pos = lambda i: (i % per_seq, 0)
    return pl.pallas_call(
        _in_proj_kernel,
        out_shape=(
            jax.ShapeDtypeStruct((t, AB_COLS), F32),
            jax.ShapeDtypeStruct((t, QK_COLS), BF16),
            jax.ShapeDtypeStruct((t // s, ATT_WIDTH, s), BF16),
        ),
        grid=(t // tm,),
        in_specs=[
            pl.BlockSpec((tm, D_MODEL), row),
            pl.BlockSpec((None, 6, D_MODEL), lambda i: (i // per_seq, 0, 0)),
            pl.BlockSpec((1, D_MODEL), lambda i: (0, 0)),
            pl.BlockSpec((D_MODEL, IN_COLS), lambda i: (0, 0)),
            pl.BlockSpec((tm, LANES), pos),
            pl.BlockSpec((tm, LANES), pos),
            pl.BlockSpec((tm, LANES), pos),
        ],
        out_specs=(
            pl.BlockSpec((tm, AB_COLS), row),
            pl.BlockSpec((tm, QK_COLS), row),
            pl.BlockSpec((None, ATT_WIDTH, tm), lambda i: (i // per_seq, 0, i % per_seq)),
        ),
        compiler_params=_cparams("arbitrary"),
        name="in_proj",
    )(x, mod, g, w_in_bf, *tables)


def _neg_expm1(y):
    series = -y * (1.0 + y * (1.0 / 2.0) * (1.0 + y * (1.0 / 3.0) * (1.0 + y * (1.0 / 4.0) * (1.0 + y * (1.0 / 5.0)))))
    return jnp.where(y > -1.0 / 64.0, series, 1.0 - jnp.exp(y))


def _tile_scan(a, b, reverse):
    n = a.shape[0]
    a3 = a.reshape(n // SUBLANES, SUBLANES, a.shape[1])
    b3 = b.reshape(n // SUBLANES, SUBLANES, b.shape[1])
    row = lax.broadcasted_iota(jnp.int32, a3.shape, 1)
    for d in (1, 2, 4):
        shift = SUBLANES - d if reverse else d
        a_s = pltpu.roll(a3, shift, 1)
        b_s = pltpu.roll(b3, shift, 1)
        m = (row < SUBLANES - d) if reverse else (row >= d)
        b3 = jnp.where(m, b3 + a3 * b_s, b3)
        a3 = jnp.where(m, a3 * a_s, a3)
    return a3.reshape(a.shape), b3.reshape(b.shape)


def _lru_kernel(cur_ref, prev_ref, next_ref, ga_ref, cw_ref, cb_ref, wa_ref, ba_ref, wx_ref, bx_ref,
                lam_ref, o_ref, hf_ref, ext_ref, a_ref, b_ref, h_ref, carry_ref, *, n_chunks):
    tc = SEQ_CHUNK
    p = pl.program_id(1)
    j = pl.program_id(2)
    jx = j + p * (n_chunks - 1 - 2 * j)

    @pl.when(j == 0)
    def _():
        carry_ref[...] = jnp.zeros_like(carry_ref)

    zero_halo = jnp.zeros((LRU_HALO, LRU_WIDTH), F32)
    ext_ref[0:LRU_HALO, :] = jnp.where(jx == 0, zero_halo, prev_ref[...])
    ext_ref[LRU_HALO:LRU_HALO + tc, :] = cur_ref[...]
    ext_ref[LRU_HALO + tc:, :] = jnp.where(jx == n_chunks - 1, zero_halo, next_ref[...])
    xc = cb_ref[...]
    for k in range(LRU_CONV):
        off = LRU_HALO - 2 + k
        xc = xc + cw_ref[k:k + 1, :] * ext_ref[off:off + tc, :]
    xb = xc.astype(BF16)
    r = jax.nn.sigmoid(jnp.dot(xb, wa_ref[...], preferred_element_type=F32) + ba_ref[...])
    gate_i = jax.nn.sigmoid(jnp.dot(xb, wx_ref[...], preferred_element_type=F32) + bx_ref[...])
    log_a = (-LRU_C) * r * jax.nn.softplus(-lam_ref[...])
    a = jnp.exp(log_a)
    b = jnp.sqrt(_neg_expm1(2.0 * log_a)) * (gate_i * xc)

    n_tiles = tc // SUBLANES

    def run(reverse):
        a_s, b_s = _tile_scan(a, b, reverse)
        a_ref[...] = a_s
        b_ref[...] = b_s

        def body(i, hc):
            ti = (n_tiles - 1 - i) if reverse else i
            off = pl.multiple_of(ti * SUBLANES, SUBLANES)
            h = b_ref[pl.ds(off, SUBLANES), :] + a_ref[pl.ds(off, SUBLANES), :] * hc
            h_ref[pl.ds(off, SUBLANES), :] = h
            edge = h[0:1, :] if reverse else h[SUBLANES - 1:SUBLANES, :]
            return jnp.broadcast_to(edge, (SUBLANES, LRU_WIDTH))

        carry_ref[...] = lax.fori_loop(0, n_tiles, body, carry_ref[...], unroll=8)

    row0 = pl.multiple_of(jx * tc, tc)

    @pl.when(p == 0)
    def _():
        run(False)
        hf_ref[pl.ds(row0, tc), :] = h_ref[...]

    @pl.when(p == 1)
    def _():
        run(True)
        hsum = hf_ref[pl.ds(row0, tc), :] + h_ref[...]
        o_ref[...] = (jax.nn.gelu(ga_ref[...]) * hsum).astype(BF16)


def _lru(ab3, conv_w, conv_b, wa_bd, ba, wx_bd, bx, lam):
    bsz, s, _ = ab3.shape
    tc = SEQ_CHUNK
    n = s // tc
    hb = tc // LRU_HALO
    jx = lambda p, j: j + p * (n - 1 - 2 * j)
    jg = lambda p, j: n - 1 - p * j
    w2 = lambda b, p, j: (0, 0)
    wdir = lambda b, p, j: (p, 0, 0)
    return pl.pallas_call(
        functools.partial(_lru_kernel, n_chunks=n),
        out_shape=jax.ShapeDtypeStruct((bsz, s, LRU_WIDTH), BF16),
        grid=(bsz, 2, n),
        in_specs=[
            pl.BlockSpec((None, tc, LRU_WIDTH), lambda b, p, j: (b, jx(p, j), 0)),
            pl.BlockSpec((None, LRU_HALO, LRU_WIDTH),
                         lambda b, p, j: (b, jnp.maximum(jx(p, j) * hb - 1, 0), 0)),
            pl.BlockSpec((None, LRU_HALO, LRU_WIDTH),
                         lambda b, p, j: (b, jnp.minimum((jx(p, j) + 1) * hb, s // LRU_HALO - 1), 0)),
            pl.BlockSpec((None, tc, LRU_WIDTH), lambda b, p, j: (b, jg(p, j), 1)),
            pl.BlockSpec((LRU_CONV, LRU_WIDTH), w2),
            pl.BlockSpec((1, LRU_WIDTH), w2),
            pl.BlockSpec((None, LRU_WIDTH, LRU_WIDTH), wdir),
            pl.BlockSpec((None, 1, LRU_WIDTH), wdir),
            pl.BlockSpec((None, LRU_WIDTH, LRU_WIDTH), wdir),
            pl.BlockSpec((None, 1, LRU_WIDTH), wdir),
            pl.BlockSpec((None, 1, LRU_WIDTH), wdir),
        ],
        out_specs=pl.BlockSpec((None, tc, LRU_WIDTH), lambda b, p, j: (b, jg(p, j), 0)),
        scratch_shapes=[
            pltpu.VMEM((s, LRU_WIDTH), F32),
            pltpu.VMEM((tc + 2 * LRU_HALO, LRU_WIDTH), F32),
            pltpu.VMEM((tc, LRU_WIDTH), F32),
            pltpu.VMEM((tc, LRU_WIDTH), F32),
            pltpu.VMEM((tc, LRU_WIDTH), F32),
            pltpu.VMEM((SUBLANES, LRU_WIDTH), F32),
        ],
        compiler_params=_cparams("arbitrary", "arbitrary", "arbitrary"),
        name="lru",
    )(ab3, ab3, ab3, ab3, conv_w, conv_b, wa_bd, ba, wx_bd, bx, lam)


def _convmod_kernel(u_ref, g_ref, up_ref, gp_ref, un_ref, gn_ref, w_ref, b_ref, lg_ref, lb_ref,
                    o_ref, ext_ref, sh_ref, *, n_chunks):
    tc = SEQ_CHUNK
    j = pl.program_id(1)
    zero_halo = jnp.zeros((CONV_HALO, CONV_WIDTH), F32)
    glu = lambda u, g: u * jax.nn.sigmoid(g)
    ext_ref[0:CONV_HALO, :] = jnp.where(j == 0, zero_halo, glu(up_ref[...], gp_ref[...]))
    ext_ref[CONV_HALO:CONV_HALO + tc, :] = glu(u_ref[...], g_ref[...])
    ext_ref[CONV_HALO + tc:, :] = jnp.where(j == n_chunks - 1, zero_halo, glu(un_ref[...], gn_ref[...]))
    span = sh_ref.shape[1]
    for ph in range(1, SUBLANES):
        sh_ref[ph] = ext_ref[ph:ph + span, :]
    acc = b_ref[...]
    for k in range(CONV_KERNEL):
        off = CONV_HALO - CONV_KERNEL // 2 + k
        ph, base = off % SUBLANES, off - off % SUBLANES
        src = ext_ref[base:base + tc, :] if ph == 0 else sh_ref[ph, base:base + tc, :]
        acc = acc + w_ref[k:k + 1, :] * src
    mu = jnp.mean(acc, axis=-1, keepdims=True)
    cen = acc - mu
    var = jnp.mean(cen * cen, axis=-1, keepdims=True)
    z = cen * lax.rsqrt(var + EPS) * lg_ref[...] + lb_ref[...]
    o_ref[...] = (z * jax.nn.sigmoid(z)).astype(BF16)


def _convmod(ab3, dw_w, dw_b, cln_g, cln_b):
    bsz, s, _ = ab3.shape
    tc = SEQ_CHUNK
    n = s // tc
    hb = tc // CONV_HALO
    w2 = lambda b, j: (0, 0)
    ucol = 2 * LRU_WIDTH // CONV_WIDTH
    gcol = ucol + 1
    cur = lambda col: pl.BlockSpec((None, tc, CONV_WIDTH), lambda b, j: (b, j, col))
    prev = lambda col: pl.BlockSpec((None, CONV_HALO, CONV_WIDTH),
                                    lambda b, j: (b, jnp.maximum(j * hb - 1, 0), col))
    nxt = lambda col: pl.BlockSpec((None, CONV_HALO, CONV_WIDTH),
                                   lambda b, j: (b, jnp.minimum((j + 1) * hb, s // CONV_HALO - 1), col))
    return pl.pallas_call(
        functools.partial(_convmod_kernel, n_chunks=n),
        out_shape=jax.ShapeDtypeStruct((bsz, s, CONV_WIDTH), BF16),
        grid=(bsz, n),
        in_specs=[
            cur(ucol), cur(gcol), prev(ucol), prev(gcol), nxt(ucol), nxt(gcol),
            pl.BlockSpec((CONV_KERNEL, CONV_WIDTH), w2),
            pl.BlockSpec((1, CONV_WIDTH), w2),
            pl.BlockSpec((1, CONV_WIDTH), w2),
            pl.BlockSpec((1, CONV_WIDTH), w2),
        ],
        out_specs=pl.BlockSpec((None, tc, CONV_WIDTH), lambda b, j: (b, j, 0)),
        scratch_shapes=[pltpu.VMEM((tc + 2 * CONV_HALO, CONV_WIDTH), F32),
                        pltpu.VMEM((SUBLANES, tc + 2 * CONV_HALO - SUBLANES, CONV_WIDTH), F32)],
        compiler_params=_cparams("arbitrary", "arbitrary"),
        name="convmod",
    )(ab3, ab3, ab3, ab3, ab3, ab3, dw_w, dw_b, cln_g, cln_b)


def _attn_kernel(q_ref, k_ref, vt_ref, lv_ref, sg_ref, o_ref, st0_ref, st1_ref, m0_ref, m1_ref, *, lam_init):
    tq = q_ref.shape[0]
    g = pl.program_id(0)

    @pl.when(g == 0)
    def _():
        st1_ref[...] = jnp.zeros_like(st1_ref)
        m1_ref[...] = jnp.zeros_like(m1_ref)

    def step(st_new_ref, m_new_ref, st_old_ref, m_old_ref):
        q = q_ref[...]
        lane = lax.broadcasted_iota(jnp.int32, q.shape, 1)
        zero = jnp.zeros_like(q)
        qq = jnp.concatenate([jnp.where(lane < ATT_QKDIM, q, zero), jnp.where(lane >= ATT_QKDIM, q, zero)], axis=0)
        kc = KEY_CHUNK
        n_chunks = k_ref.shape[0] // kc
        ones = jnp.ones((ONES_ROWS, kc), BF16)
        m_old = m_old_ref[...]
        chunk = lambda c: slice(c * kc, (c + 1) * kc)

        def scores(c, m_new):
            st_c = lax.dot_general(k_ref[chunk(c), :], qq, (((1,), (1,)), ((), ())), preferred_element_type=F32)
            st_new_ref[chunk(c), :] = st_c
            m_c = jnp.max(st_c, axis=0, keepdims=True)
            return m_c if m_new is None else jnp.maximum(m_new, m_c)

        m_new = scores(0, None)
        ovt = None
        for c in range(n_chunks):
            e_c = jnp.exp(st_old_ref[chunk(c), :] - m_old).astype(BF16)
            if c + 1 < n_chunks:
                m_new = scores(c + 1, m_new)
            vt_c = jnp.concatenate([vt_ref[:, chunk(c)], ones], axis=0)
            o_c = jnp.dot(vt_c, e_c, preferred_element_type=F32)
            ovt = o_c if ovt is None else ovt + o_c
        m_new_ref[...] = m_new
        ovt = ovt[0:ATT_VDIM, :] / ovt[ATT_VDIM:ATT_VDIM + 1, :]
        lv = lv_ref[...]
        lam = (jnp.exp(jnp.sum(lv[0:1, :] * lv[1:2, :], axis=-1, keepdims=True))
               - jnp.exp(jnp.sum(lv[2:3, :] * lv[3:4, :], axis=-1, keepdims=True)) + lam_init)
        ot = ovt[:, 0:tq] - lam * ovt[:, tq:]
        ot = ot * lax.rsqrt(jnp.mean(ot * ot, axis=0, keepdims=True) + EPS) * sg_ref[...] * (1.0 - lam_init)
        o_ref[...] = ot.T.astype(BF16)

    @pl.when(g % 2 == 0)
    def _():
        step(st0_ref, m0_ref, st1_ref, m1_ref)

    @pl.when(g % 2 == 1)
    def _():
        step(st1_ref, m1_ref, st0_ref, m0_ref)


def _attn(qk3, vt3, lam_vec, subln_g, lam_init):
    bsz, s, _ = qk3.shape
    tq = min(MAX_Q_TILE, SCORE_TILE_BYTES // (2 * s * 4))
    n_q = s // tq
    n_tiles = bsz * ATT_HEADS * n_q
    kcol = ATT_WIDTH // LANES
    w2 = lambda g: (0, 0)
    cur = lambda g: jnp.minimum(g, n_tiles - 1)
    prv = lambda g: jnp.maximum(g - 1, 0)
    bat = lambda t: t // (ATT_HEADS * n_q)
    head = lambda t: (t // n_q) % ATT_HEADS
    qi = lambda t: t % n_q
    return pl.pallas_call(
        functools.partial(_attn_kernel, lam_init=lam_init),
        out_shape=jax.ShapeDtypeStruct((bsz, s, ATT_WIDTH), BF16),
        grid=(n_tiles + 1,),
        in_specs=[
            pl.BlockSpec((None, tq, LANES), lambda g: (bat(cur(g)), qi(cur(g)), head(cur(g)))),
            pl.BlockSpec((None, s, LANES), lambda g: (bat(cur(g)), 0, kcol + head(cur(g)))),
            pl.BlockSpec((None, ATT_VDIM, s), lambda g: (bat(prv(g)), head(prv(g)), 0)),
            pl.BlockSpec((4, ATT_QKDIM), w2),
            pl.BlockSpec((ATT_VDIM, 1), w2),
        ],
        out_specs=pl.BlockSpec((None, tq, ATT_VDIM), lambda g: (bat(prv(g)), qi(prv(g)), head(prv(g)))),
        scratch_shapes=[pltpu.VMEM((s, 2 * tq), F32), pltpu.VMEM((s, 2 * tq), F32),
                        pltpu.VMEM((1, 2 * tq), F32), pltpu.VMEM((1, 2 * tq), F32)],
        compiler_params=_cparams("arbitrary"),
        name="attn",
    )(qk3, qk3, vt3, lam_vec, subln_g.reshape(ATT_VDIM, 1))


def _first_argmax(vals):
    best = vals[0]
    idx = jnp.zeros(best.shape, jnp.int32)
    for i in range(1, len(vals)):
        better = vals[i] > best
        idx = jnp.where(better, i, idx)
        best = jnp.where(better, vals[i], best)
    return idx, best


def _out_proj_kernel(x_ref, ya_ref, yb_ref, yc_ref, w_ref, mod_ref, g_ref, rw_ref, rb_ref,
                     xo_ref, hp_ref, meta_ref, cnt_ref):
    tm = TOKEN_TILE
    y = jnp.dot(ya_ref[...], w_ref[0:LRU_WIDTH, :], preferred_element_type=F32)
    y = y + jnp.dot(yb_ref[...], w_ref[LRU_WIDTH:LRU_WIDTH + CONV_WIDTH, :], preferred_element_type=F32)
    y = y + jnp.dot(yc_ref[...], w_ref[LRU_WIDTH + CONV_WIDTH:, :], preferred_element_type=F32)
    x = x_ref[...] + mod_ref[2:3, :] * y
    xo_ref[...] = x
    h = _rms(x, g_ref[...]) * (1.0 + mod_ref[4:5, :]) + mod_ref[3:4, :]
    hp_ref[:, 0:D_MODEL] = h

    logits = lax.dot_general(rw_ref[...], h, (((1,), (1,)), ((), ())), precision=lax.Precision.HIGHEST,
                             preferred_element_type=F32) + rb_ref[...]
    gl = [logits[g:g + 1, :] for g in range(N_GROUPS)]
    g_idx, g_max = _first_argmax(gl)
    denom = gl[0] * 0.0
    for g in range(N_GROUPS):
        denom = denom + jnp.exp(gl[g] - g_max)
    g_w = 1.0 / denom
    fl = []
    for e in range(EXPERTS_PER_GROUP):
        sel = logits[N_GROUPS + e:N_GROUPS + e + 1, :]
        for g in range(1, N_GROUPS):
            r0 = N_GROUPS + g * EXPERTS_PER_GROUP + e
            sel = jnp.where(g_idx == g, logits[r0:r0 + 1, :], sel)
        fl.append(sel)
    i1, v1 = _first_argmax(fl)
    neg = jnp.full(v1.shape, -jnp.inf, F32)
    i2, v2 = _first_argmax([jnp.where(i1 == e, neg, fl[e]) for e in range(EXPERTS_PER_GROUP)])
    t = jnp.exp(v2 - v1)
    p1 = g_w / (1.0 + t)
    p2 = g_w * t / (1.0 + t)
    first_lo = i1 < i2
    lo = jnp.where(first_lo, i1, i2)
    hi = jnp.where(first_lo, i2, i1)
    pair = jnp.where(lo == 0, 0, jnp.where(lo == 1, 3, 5)) + hi - lo - 1
    bucket = g_idx * PAIRS_PER_GROUP + pair
    p_lo = jnp.where(first_lo, p1, p2)
    p_hi = jnp.where(first_lo, p2, p1)
    wrows = jnp.concatenate([p_lo, p_hi, jnp.zeros((LANES - 2, tm), F32)], axis=0)
    hp_ref[:, D_MODEL:] = wrows.T

    @pl.when(pl.program_id(0) == 0)
    def _():
        cnt_ref[...] = jnp.zeros_like(cnt_ref)

    onehot = lax.broadcasted_iota(jnp.int32, (BUCKET_ROWS, tm), 0) == bucket
    before = lax.broadcasted_iota(jnp.int32, (tm, tm), 0) < lax.broadcasted_iota(jnp.int32, (tm, tm), 1)
    prefix = jnp.dot(jnp.where(onehot, 1.0, 0.0).astype(BF16), jnp.where(before, 1.0, 0.0).astype(BF16),
                     preferred_element_type=F32)
    base = cnt_ref[:, 0:1]
    rank = jnp.sum(jnp.where(onehot, prefix + base, 0.0), axis=0, keepdims=True)
    cnt_ref[...] = cnt_ref[...] + jnp.sum(jnp.where(onehot, 1.0, 0.0), axis=1, keepdims=True)
    meta_ref[...] = jnp.concatenate([bucket.astype(F32), rank, jnp.zeros((SUBLANES - 2, tm), F32)], axis=0)


def _out_proj(x, ya, yb, yc, w_out_bf, mod, g2, rw_t, rb, s):
    t = x.shape[0]
    tm = TOKEN_TILE
    per_seq = s // tm
    row = lambda i: (i, 0)
    w2 = lambda i: (0, 0)
    return pl.pallas_call(
        _out_proj_kernel,
        out_shape=(
            jax.ShapeDtypeStruct((t, D_MODEL), F32),
            jax.ShapeDtypeStruct((t, PAYLOAD_COLS), F32),
            jax.ShapeDtypeStruct((SUBLANES, t), F32),
            jax.ShapeDtypeStruct((BUCKET_ROWS, LANES), F32),
        ),
        grid=(t // tm,),
        in_specs=[
            pl.BlockSpec((tm, D_MODEL), row),
            pl.BlockSpec((tm, LRU_WIDTH), row),
            pl.BlockSpec((tm, CONV_WIDTH), row),
            pl.BlockSpec((tm, ATT_WIDTH), row),
            pl.BlockSpec((D_MODEL, D_MODEL), w2),
            pl.BlockSpec((None, 6, D_MODEL), lambda i: (i // per_seq, 0, 0)),
            pl.BlockSpec((1, D_MODEL), w2),
            pl.BlockSpec((ROUTER_ROWS, D_MODEL), w2),
            pl.BlockSpec((ROUTER_ROWS, 1), w2),
        ],
        out_specs=(
            pl.BlockSpec((tm, D_MODEL), row),
            pl.BlockSpec((tm, PAYLOAD_COLS), row),
            pl.BlockSpec((SUBLANES, tm), lambda i: (0, i)),
            pl.BlockSpec((BUCKET_ROWS, LANES), w2),
        ),
        compiler_params=_cparams("arbitrary"),
        name="out_proj",
    )(x, ya, yb, yc, w_out_bf, mod, g2, rw_t, rb)


def _route_tables(meta, cnt, t):
    tm = TOKEN_TILE
    n_tiles = t // tm + N_BUCKETS
    counts = cnt[:N_BUCKETS, 0].astype(jnp.int32)
    tiles_per = (counts + tm - 1) // tm
    tile_end = jnp.cumsum(tiles_per)
    offs = (tile_end - tiles_per) * tm
    n_used = tile_end[-1:]
    bucket = meta[0].astype(jnp.int32)
    rank = meta[1].astype(jnp.int32)
    ids = jnp.arange(N_BUCKETS, dtype=jnp.int32)
    dest = rank + jnp.sum(jnp.where(bucket[:, None] == ids[None, :], offs[None, :], 0), axis=1)
    slot_token = jnp.full(((n_tiles + 1) * tm,), -1, jnp.int32).at[dest].set(jnp.arange(t, dtype=jnp.int32))
    tile = jnp.minimum(jnp.arange(n_tiles + 2, dtype=jnp.int32), n_used - 1)
    tb = jnp.sum((tile[:, None] >= tile_end[None, :]).astype(jnp.int32), axis=1)
    pair_lo = jnp.array([0, 0, 0, 1, 1, 2], jnp.int32)
    pair_hi = jnp.array([1, 2, 3, 2, 3, 3], jnp.int32)
    onehot_pair = (tb % PAIRS_PER_GROUP)[:, None] == jnp.arange(PAIRS_PER_GROUP, dtype=jnp.int32)[None, :]
    e_lo = (tb // PAIRS_PER_GROUP) * EXPERTS_PER_GROUP + jnp.sum(jnp.where(onehot_pair, pair_lo[None, :], 0), axis=1)
    e_hi = (tb // PAIRS_PER_GROUP) * EXPERTS_PER_GROUP + jnp.sum(jnp.where(onehot_pair, pair_hi[None, :], 0), axis=1)
    return slot_token, jnp.stack([e_lo, e_hi], axis=1).reshape(-1), n_used, n_tiles


def _row_copy(src_ref, src_row, dst_ref, dst_row, sem):
    return pltpu.make_async_copy(src_ref.at[pl.ds(src_row, 1), :], dst_ref.at[pl.ds(dst_row, 1), :], sem)


def _experts_kernel(eid_ref, nused_ref, tok_ref, hp_ref, w13a_ref, w2a_ref, w13b_ref, w2b_ref, y_ref,
                    x0_ref, x1_ref, o0_ref, o1_ref, gsem, ssem, *, n_tokens):
    del eid_ref
    tm = TOKEN_TILE
    i = pl.program_id(0)
    n_used = nused_ref[0]
    even = i % 2 == 0

    def gather_start(tile, r, x_ref):
        tok = jnp.maximum(tok_ref[tile * tm + r], 0)
        _row_copy(hp_ref, tok, x_ref, r, gsem).start()

    def scatter_start(tile, r, o_ref, spare):
        tok = tok_ref[tile * tm + r]
        row = jnp.where(jnp.logical_or(tok < 0, spare), n_tokens + r, tok)
        _row_copy(o_ref, r, y_ref, row, ssem).start()

    @pl.when(i == 0)
    def _():
        o1_ref[...] = jnp.zeros_like(o1_ref)

        def issue(r, carry):
            gather_start(0, r, x0_ref)
            return carry

        lax.fori_loop(0, tm, issue, 0, unroll=8)

    @pl.when(jnp.logical_and(i >= 1, i <= n_used + 1))
    def _():
        pltpu.make_async_copy(o0_ref, y_ref.at[pl.ds(0, tm), :], ssem).wait()

    @pl.when(i <= n_used)
    def _():
        pltpu.make_async_copy(hp_ref.at[pl.ds(0, tm), :], x0_ref, gsem).wait()

    def busy_step(x_ref, x_next_ref, o_ref, o_prev_ref):
        prev = jnp.maximum(i - 1, 0)
        for r in range(tm):
            gather_start(i + 1, r, x_next_ref)
        for r in range(tm):
            scatter_start(prev, r, o_prev_ref, i == 0)
        x = x_ref[:, 0:D_MODEL].astype(BF16)

        def expert(w13_ref, w2_ref):
            h13 = jnp.dot(x, w13_ref[...], preferred_element_type=F32)
            a = h13[:, 0:D_EXPERT]
            he = (a * jax.nn.sigmoid(a)) * h13[:, D_EXPERT:]
            return jnp.dot(he.astype(BF16), w2_ref[...], preferred_element_type=F32)

        o_ref[...] = (x_ref[:, D_MODEL:D_MODEL + 1] * expert(w13a_ref, w2a_ref)
                      + x_ref[:, D_MODEL + 1:D_MODEL + 2] * expert(w13b_ref, w2b_ref))

    busy = i < n_used

    @pl.when(jnp.logical_and(busy, even))
    def _():
        busy_step(x0_ref, x1_ref, o0_ref, o1_ref)

    @pl.when(jnp.logical_and(busy, jnp.logical_not(even)))
    def _():
        busy_step(x1_ref, x0_ref, o1_ref, o0_ref)

    def last_scatter(o_prev_ref):
        def issue(r, carry):
            scatter_start(i - 1, r, o_prev_ref, False)
            return carry

        lax.fori_loop(0, tm, issue, 0, unroll=8)

    @pl.when(jnp.logical_and(i == n_used, even))
    def _():
        last_scatter(o1_ref)

    @pl.when(jnp.logical_and(i == n_used, jnp.logical_not(even)))
    def _():
        last_scatter(o0_ref)


def _experts(slot_token, eids, n_used, n_tiles, hp, w13, w2):
    t = hp.shape[0]
    tm = TOKEN_TILE
    wspec = lambda shape, slot: pl.BlockSpec((None,) + shape, lambda i, eid, nu, tok: (eid[2 * i + slot], 0, 0))
    return pl.pallas_call(
        functools.partial(_experts_kernel, n_tokens=t),
        out_shape=jax.ShapeDtypeStruct((t + tm, D_MODEL), F32),
        grid_spec=pltpu.PrefetchScalarGridSpec(
            num_scalar_prefetch=3,
            grid=(n_tiles + 2,),
            in_specs=[
                pl.BlockSpec(memory_space=pl.ANY),
                wspec((D_MODEL, 2 * D_EXPERT), 0),
                wspec((D_EXPERT, D_MODEL), 0),
                wspec((D_MODEL, 2 * D_EXPERT), 1),
                wspec((D_EXPERT, D_MODEL), 1),
            ],
            out_specs=pl.BlockSpec(memory_space=pl.ANY),
            scratch_shapes=[
                pltpu.VMEM((tm, PAYLOAD_COLS), F32), pltpu.VMEM((tm, PAYLOAD_COLS), F32),
                pltpu.VMEM((tm, D_MODEL), F32), pltpu.VMEM((tm, D_MODEL), F32),
                pltpu.SemaphoreType.DMA, pltpu.SemaphoreType.DMA,
            ],
        ),
        compiler_params=_cparams("arbitrary"),
        name="experts",
    )(eids, n_used, slot_token, hp, w13, w2, w13, w2)


def _resid_kernel(x_ref, y_ref, mod_ref, g_ref, o_ref, *, final):
    x = x_ref[...] + mod_ref[5:6, :] * y_ref[...]
    o_ref[...] = _rms(x, g_ref[...]) if final else x


def _resid(x, y, mod, final_g, s, final):
    t = x.shape[0]
    tm = TOKEN_TILE
    per_seq = s // tm
    row = lambda i: (i, 0)
    return pl.pallas_call(
        functools.partial(_resid_kernel, final=final),
        out_shape=jax.ShapeDtypeStruct((t, D_MODEL), F32),
        grid=(t // tm,),
        in_specs=[
            pl.BlockSpec((tm, D_MODEL), row),
            pl.BlockSpec((tm, D_MODEL), row),
            pl.BlockSpec((None, 6, D_MODEL), lambda i: (i // per_seq, 0, 0)),
            pl.BlockSpec((1, D_MODEL), lambda i: (0, 0)),
        ],
        out_specs=pl.BlockSpec((tm, D_MODEL), row),
        compiler_params=_cparams("arbitrary"),
        name="resid",
    )(x, y, mod, final_g)


def _block_diag(w):
    h, dh, _ = w.shape
    eye = jnp.eye(h, dtype=w.dtype)
    return (eye[:, None, :, None] * w[:, :, None, :]).reshape(h * dh, h * dh)


def _prep_layer(l, p):
    row = lambda a: a.reshape(1, -1)
    rw = jnp.concatenate([p["router_w1"][l], p["router_w2"][l].reshape(D_MODEL, N_EXPERTS)], axis=1)
    rw = jnp.pad(rw, ((0, 0), (0, ROUTER_ROWS - rw.shape[1])))
    rb = jnp.concatenate([p["router_b1"][l], p["router_b2"][l].reshape(N_EXPERTS)])
    rb = jnp.pad(rb, (0, ROUTER_ROWS - rb.shape[0]))
    return dict(
        norm1_g=row(p["norm1_g"][l]),
        norm2_g=row(p["norm2_g"][l]),
        w_in=p["w_in"][l].astype(BF16),
        w_out=p["w_out"][l].astype(BF16),
        conv_w=p["conv_w"][l],
        conv_b=row(p["conv_b"][l]),
        wa_bd=jnp.stack([_block_diag(p["lru_wa"][l, d]) for d in range(2)]).astype(BF16),
        wx_bd=jnp.stack([_block_diag(p["lru_wx"][l, d]) for d in range(2)]).astype(BF16),
        ba=p["lru_ba"][l].reshape(2, 1, LRU_WIDTH),
        bx=p["lru_bx"][l].reshape(2, 1, LRU_WIDTH),
        lam=p["lru_lambda"][l].reshape(2, 1, LRU_WIDTH),
        dw_w=p["dw_w"][l],
        dw_b=row(p["dw_b"][l]),
        cln_g=row(p["cln_g"][l]),
        cln_b=row(p["cln_b"][l]),
        lam_vec=p["lam_vec"][l],
        subln_g=row(p["subln_g"][l]),
        rw_t=rw.T,
        rb=rb.reshape(ROUTER_ROWS, 1),
        w13=jnp.concatenate([p["moe_w1"][l], p["moe_w3"][l]], axis=-1).astype(BF16),
        w2=p["moe_w2"][l].astype(BF16),
    )


def _trunk(x, mods, layers, final_g):
    bsz, s, _ = x.shape
    t = bsz * s
    tables = _rope_tables(s)
    x = x.reshape(t, D_MODEL)
    for l, w in enumerate(layers):
        mod = mods[l]
        ab, qk, vt = _in_proj(x, mod, w["norm1_g"], w["w_in"], tables, s)
        ab3 = ab.reshape(bsz, s, AB_COLS)
        ya = _lru(ab3, w["conv_w"], w["conv_b"], w["wa_bd"], w["ba"], w["wx_bd"], w["bx"], w["lam"])
        yb = _convmod(ab3, w["dw_w"], w["dw_b"], w["cln_g"], w["cln_b"])
        lam_init = 0.8 - 0.6 * math.exp(-0.3 * l)
        yc = _attn(qk.reshape(bsz, s, QK_COLS), vt, w["lam_vec"], w["subln_g"], lam_init)
        x, hp, meta, cnt = _out_proj(x, ya.reshape(t, LRU_WIDTH), yb.reshape(t, CONV_WIDTH),
                                     yc.reshape(t, ATT_WIDTH), w["w_out"], mod, w["norm2_g"], w["rw_t"], w["rb"], s)
        slot_token, eids, n_used, n_tiles = _route_tables(meta, cnt, t)
        y = _experts(slot_token, eids, n_used, n_tiles, hp, w["w13"], w["w2"])
        x = _resid(x, y, mod, final_g, s, final=(l == DEPTH - 1))
    return x.reshape(bsz, s, D_MODEL)


def kernel(x_prompt, x_sample, c_prompt, c_sample, norm1_g, norm2_g, final_g, ada_w, ada_b, w_in, w_out, conv_w, conv_b, lru_wa, lru_ba, lru_wx, lru_bx, lru_lambda, dw_w, dw_b, cln_g, cln_b, lam_vec, subln_g, router_w1, router_b1, router_w2, router_b2, moe_w1, moe_w3, moe_w2):
    p = dict(norm1_g=norm1_g, norm2_g=norm2_g, w_in=w_in, w_out=w_out, conv_w=conv_w, conv_b=conv_b,
             lru_wa=lru_wa, lru_ba=lru_ba, lru_wx=lru_wx, lru_bx=lru_bx, lru_lambda=lru_lambda,
             dw_w=dw_w, dw_b=dw_b, cln_g=cln_g, cln_b=cln_b, lam_vec=lam_vec, subln_g=subln_g,
             router_w1=router_w1, router_b1=router_b1, router_w2=router_w2, router_b2=router_b2,
             moe_w1=moe_w1, moe_w3=moe_w3, moe_w2=moe_w2)
    layers = [_prep_layer(l, p) for l in range(DEPTH)]
    nb = c_prompt.shape[0]
    mods = _ada_mod(jnp.concatenate([c_prompt, c_sample], axis=0), ada_w, ada_b)
    fg = final_g.reshape(1, D_MODEL)
    y_prompt = _trunk(x_prompt, mods[:, :nb], layers, fg)
    y_sample = _trunk(x_sample, mods[:, nb:], layers, fg)
    return (y_prompt, y_sample)
```

```python
import functools
import math

import jax
import jax.numpy as jnp
from jax import lax
from jax.experimental import pallas as pl
from jax.experimental.pallas import tpu as pltpu

F32 = jnp.float32
BF16 = jnp.bfloat16

D_MODEL = 1024
DEPTH = 2
LRU_WIDTH = 256
LRU_HEADS = 4
LRU_CONV = 4
LRU_C = 8.0
CONV_WIDTH = 256
CONV_KERNEL = 31
ATT_WIDTH = 512
ATT_HEADS = 4
ATT_VDIM = 128
ATT_QKDIM = 64
ROPE_DIM = 16
ROPE_THETA = 500000.0
AB_COLS = 2 * LRU_WIDTH + 2 * CONV_WIDTH
QK_COLS = 2 * ATT_WIDTH
IN_COLS = AB_COLS + QK_COLS + ATT_WIDTH
N_GROUPS = 4
EXPERTS_PER_GROUP = 4
N_EXPERTS = 16
D_EXPERT = 512
EPS = 1e-6

LANES = 128
SUBLANES = 8
ROUTER_ROWS = 24
PAIRS_PER_GROUP = 6
N_BUCKETS = N_GROUPS * PAIRS_PER_GROUP
BUCKET_ROWS = 32
PAYLOAD_COLS = D_MODEL + LANES
VMEM_LIMIT = 56 * 1024 * 1024

TOKEN_TILE = 512
SEQ_CHUNK = 512
SCORE_TILE_BYTES = 16 * 1024 * 1024
MAX_Q_TILE = 512
ONES_ROWS = 16
KEY_CHUNK = 256
LRU_HALO = 8
CONV_HALO = 16


def _cparams(*sem):
    return pltpu.CompilerParams(dimension_semantics=sem, vmem_limit_bytes=VMEM_LIMIT)


def _rms(x, g):
    return x * lax.rsqrt(jnp.mean(x * x, axis=-1, keepdims=True) + EPS) * g


def _ada_kernel(c_ref, w_ref, b_ref, o_ref):
    c = c_ref[...]
    cs = (c * jax.nn.sigmoid(c)).astype(BF16)
    o_ref[...] = jnp.dot(cs, w_ref[...].astype(BF16), preferred_element_type=F32) + b_ref[...]


def _ada_mod(c, ada_w, ada_b):
    nb = c.shape[0]
    tn = 1536
    out = pl.pallas_call(
        _ada_kernel,
        out_shape=jax.ShapeDtypeStruct((DEPTH, nb, 6 * D_MODEL), F32),
        grid=(DEPTH, 6 * D_MODEL // tn),
        in_specs=[
            pl.BlockSpec((nb, D_MODEL), lambda l, j: (0, 0)),
            pl.BlockSpec((None, D_MODEL, tn), lambda l, j: (l, 0, j)),
            pl.BlockSpec((None, 1, tn), lambda l, j: (l, 0, j)),
        ],
        out_specs=pl.BlockSpec((None, nb, tn), lambda l, j: (l, 0, j)),
        compiler_params=_cparams("arbitrary", "arbitrary"),
        name="ada_mod",
    )(c, ada_w, ada_b.reshape(DEPTH, 1, 6 * D_MODEL))
    return out.reshape(DEPTH, nb, 6, D_MODEL)


def _in_proj_kernel(x_ref, mod_ref, g_ref, w_ref, cos_ref, sa_ref, sb_ref, ab_ref, qk_ref, vt_ref):
    x = x_ref[...]
    h = _rms(x, g_ref[...]) * (1.0 + mod_ref[1:2, :]) + mod_ref[0:1, :]
    hb = h.astype(BF16)
    ab_ref[...] = jnp.dot(hb, w_ref[:, 0:AB_COLS], preferred_element_type=F32)
    qk = jnp.dot(hb, w_ref[:, AB_COLS:AB_COLS + QK_COLS], preferred_element_type=F32)
    cos, sa, sb = cos_ref[...], sa_ref[...], sb_ref[...]
    for c in range(QK_COLS // LANES):
        blk = qk[:, c * LANES:(c + 1) * LANES]
        rot = (blk * cos + pltpu.roll(blk, ROPE_DIM // 2, 1) * sa
               + pltpu.roll(blk, LANES - ROPE_DIM // 2, 1) * sb)
        if c < ATT_WIDTH // LANES:
            rot = rot * (ATT_QKDIM ** -0.5)
        qk_ref[:, c * LANES:(c + 1) * LANES] = rot.astype(BF16)
    v = jnp.dot(hb, w_ref[:, AB_COLS + QK_COLS:], preferred_element_type=F32)
    vt_ref[...] = v.T.astype(BF16)


def _rope_tables(s):
    half = ROPE_DIM // 2
    inv = ROPE_THETA ** (-(jnp.arange(half, dtype=F32) * 2.0) / ROPE_DIM)
    ang = jnp.arange(s, dtype=jnp.int32).astype(F32)[:, None] * inv[None, :]
    cos, sin = jnp.cos(ang), jnp.sin(ang)
    rest = ATT_QKDIM - ROPE_DIM
    one = jnp.ones((s, rest), F32)
    zero = jnp.zeros((s, rest), F32)
    zh = jnp.zeros((s, half), F32)
    reps = LANES // ATT_QKDIM
    cos_t = jnp.tile(jnp.concatenate([cos, cos, one], axis=1), (1, reps))
    sa_t = jnp.tile(jnp.concatenate([zh, sin, zero], axis=1), (1, reps))
    sb_t = jnp.tile(jnp.concatenate([-sin, zh, zero], axis=1), (1, reps))
    return cos_t, sa_t, sb_t


def _in_proj(x, mod, g, w_in_bf, tables, s):
    t = x.shape[0]
    tm = TOKEN_TILE
    per_seq = s // tm
    row = lambda i: (i, 0)
    pos = lambda i: (i % per_seq, 0)
    return pl.pallas_call(
        _in_proj_kernel,
        out_shape=(
            jax.ShapeDtypeStruct((t, AB_COLS), F32),
            jax.ShapeDtypeStruct((t, QK_COLS), BF16),
            jax.ShapeDtypeStruct((t // s, ATT_WIDTH, s), BF16),
        ),
        grid=(t // tm,),
        in_specs=[
            pl.BlockSpec((tm, D_MODEL), row),
            pl.BlockSpec((None, 6, D_MODEL), lambda i: (i // per_seq, 0, 0)),
            pl.BlockSpec((1, D_MODEL), lambda i: (0, 0)),
            pl.BlockSpec((D_MODEL, IN_COLS), lambda i: (0, 0)),
            pl.BlockSpec((tm, LANES), pos),
            pl.BlockSpec((tm, LANES), pos),
            pl.BlockSpec((tm, LANES), pos),
        ],
        out_specs=(
            pl.BlockSpec((tm, AB_COLS), row),
            pl.BlockSpec((tm, QK_COLS), row),
            pl.BlockSpec((None, ATT_WIDTH, tm), lambda i: (i // per_seq, 0, i % per_seq)),
        ),
        compiler_params=_cparams("arbitrary"),
        name="in_proj",
    )(x, mod, g, w_in_bf, *tables)


def _neg_expm1(y):
    series = -y * (1.0 + y * (1.0 / 2.0) * (1.0 + y * (1.0 / 3.0) * (1.0 + y * (1.0 / 4.0) * (1.0 + y * (1.0 / 5.0)))))
    return jnp.where(y > -1.0 / 64.0, series, 1.0 - jnp.exp(y))


def _tile_scan(a, b, reverse):
    n = a.shape[0]
    a3 = a.reshape(n // SUBLANES, SUBLANES, a.shape[1])
    b3 = b.reshape(n // SUBLANES, SUBLANES, b.shape[1])
    row = lax.broadcasted_iota(jnp.int32, a3.shape, 1)
    for d in (1, 2, 4):
        shift = SUBLANES - d if reverse else d
        a_s = pltpu.roll(a3, shift, 1)
        b_s = pltpu.roll(b3, shift, 1)
        m = (row < SUBLANES - d) if reverse else (row >= d)
        b3 = jnp.where(m, b3 + a3 * b_s, b3)
        a3 = jnp.where(m, a3 * a_s, a3)
    return a3.reshape(a.shape), b3.reshape(b.shape)


def _lru_kernel(cur_ref, prev_ref, next_ref, ga_ref, cw_ref, cb_ref, wa_ref, ba_ref, wx_ref, bx_ref,
                lam_ref, o_ref, hf_ref, ext_ref, a_ref, b_ref, h_ref, carry_ref, *, n_chunks):
    tc = SEQ_CHUNK
    p = pl.program_id(1)
    j = pl.program_id(2)
    jx = j + p * (n_chunks - 1 - 2 * j)

    @pl.when(j == 0)
    def _():
        carry_ref[...] = jnp.zeros_like(carry_ref)

    zero_halo = jnp.zeros((LRU_HALO, LRU_WIDTH), F32)
    ext_ref[0:LRU_HALO, :] = jnp.where(jx == 0, zero_halo, prev_ref[...])
    ext_ref[LRU_HALO:LRU_HALO + tc, :] = cur_ref[...]
    ext_ref[LRU_HALO + tc:, :] = jnp.where(jx == n_chunks - 1, zero_halo, next_ref[...])
    xc = cb_ref[...]
    for k in range(LRU_CONV):
        off = LRU_HALO - 2 + k
        xc = xc + cw_ref[k:k + 1, :] * ext_ref[off:off + tc, :]
    xb = xc.astype(BF16)
    r = jax.nn.sigmoid(jnp.dot(xb, wa_ref[...], preferred_element_type=F32) + ba_ref[...])
    gate_i = jax.nn.sigmoid(jnp.dot(xb, wx_ref[...], preferred_element_type=F32) + bx_ref[...])
    log_a = (-LRU_C) * r * jax.nn.softplus(-lam_ref[...])
    a = jnp.exp(log_a)
    b = jnp.sqrt(_neg_expm1(2.0 * log_a)) * (gate_i * xc)

    n_tiles = tc // SUBLANES

    def run(reverse):
        a_s, b_s = _tile_scan(a, b, reverse)
        a_ref[...] = a_s
        b_ref[...] = b_s

        def body(i, hc):
            ti = (n_tiles - 1 - i) if reverse else i
            off = pl.multiple_of(ti * SUBLANES, SUBLANES)
            h = b_ref[pl.ds(off, SUBLANES), :] + a_ref[pl.ds(off, SUBLANES), :] * hc
            h_ref[pl.ds(off, SUBLANES), :] = h
            edge = h[0:1, :] if reverse else h[SUBLANES - 1:SUBLANES, :]
            return jnp.broadcast_to(edge, (SUBLANES, LRU_WIDTH))

        carry_ref[...] = lax.fori_loop(0, n_tiles, body, carry_ref[...], unroll=8)

    row0 = pl.multiple_of(jx * tc, tc)

    @pl.when(p == 0)
    def _():
        run(False)
        hf_ref[pl.ds(row0, tc), :] = h_ref[...]

    @pl.when(p == 1)
    def _():
        run(True)
        hsum = hf_ref[pl.ds(row0, tc), :] + h_ref[...]
        o_ref[...] = (jax.nn.gelu(ga_ref[...]) * hsum).astype(BF16)


def _lru(ab3, conv_w, conv_b, wa_bd, ba, wx_bd, bx, lam):
    bsz, s, _ = ab3.shape
    tc = SEQ_CHUNK
    n = s // tc
    hb = tc // LRU_HALO
    jx = lambda p, j: j + p * (n - 1 - 2 * j)
    jg = lambda p, j: n - 1 - p * j
    w2 = lambda b, p, j: (0, 0)
    wdir = lambda b, p, j: (p, 0, 0)
    return pl.pallas_call(
        functools.partial(_lru_kernel, n_chunks=n),
        out_shape=jax.ShapeDtypeStruct((bsz, s, LRU_WIDTH), BF16),
        grid=(bsz, 2, n),
        in_specs=[
            pl.BlockSpec((None, tc, LRU_WIDTH), lambda b, p, j: (b, jx(p, j), 0)),
            pl.BlockSpec((None, LRU_HALO, LRU_WIDTH),
                         lambda b, p, j: (b, jnp.maximum(jx(p, j) * hb - 1, 0), 0)),
            pl.BlockSpec((None, LRU_HALO, LRU_WIDTH),
                         lambda b, p, j: (b, jnp.minimum((jx(p, j) + 1) * hb, s // LRU_HALO - 1), 0)),
            pl.BlockSpec((None, tc, LRU_WIDTH), lambda b, p, j: (b, jg(p, j), 1)),
            pl.BlockSpec((LRU_CONV, LRU_WIDTH), w2),
            pl.BlockSpec((1, LRU_WIDTH), w2),
            pl.BlockSpec((None, LRU_WIDTH, LRU_WIDTH), wdir),
            pl.BlockSpec((None, 1, LRU_WIDTH), wdir),
            pl.BlockSpec((None, LRU_WIDTH, LRU_WIDTH), wdir),
            pl.BlockSpec((None, 1, LRU_WIDTH), wdir),
            pl.BlockSpec((None, 1, LRU_WIDTH), wdir),
        ],
        out_specs=pl.BlockSpec((None, tc, LRU_WIDTH), lambda b, p, j: (b, jg(p, j), 0)),
        scratch_shapes=[
            pltpu.VMEM((s, LRU_WIDTH), F32),
            pltpu.VMEM((tc + 2 * LRU_HALO, LRU_WIDTH), F32),
            pltpu.VMEM((tc, LRU_WIDTH), F32),
            pltpu.VMEM((tc, LRU_WIDTH), F32),
            pltpu.VMEM((tc, LRU_WIDTH), F32),
            pltpu.VMEM((SUBLANES, LRU_WIDTH), F32),
        ],
        compiler_params=_cparams("arbitrary", "arbitrary", "arbitrary"),
        name="lru",
    )(ab3, ab3, ab3, ab3, conv_w, conv_b, wa_bd, ba, wx_bd, bx, lam)


def _convmod_kernel(u_ref, g_ref, up_ref, gp_ref, un_ref, gn_ref, w_ref, b_ref, lg_ref, lb_ref,
                    o_ref, ext_ref, sh_ref, *, n_chunks):
    tc = SEQ_CHUNK
    j = pl.program_id(1)
    zero_halo = jnp.zeros((CONV_HALO, CONV_WIDTH), F32)
    glu = lambda u, g: u * jax.nn.sigmoid(g)
    ext_ref[0:CONV_HALO, :] = jnp.where(j == 0, zero_halo, glu(up_ref[...], gp_ref[...]))
    ext_ref[CONV_HALO:CONV_HALO + tc, :] = glu(u_ref[...], g_ref[...])
    ext_ref[CONV_HALO + tc:, :] = jnp.where(j == n_chunks - 1, zero_halo, glu(un_ref[...], gn_ref[...]))
    span = sh_ref.shape[1]
    for ph in range(1, SUBLANES):
        sh_ref[ph] = ext_ref[ph:ph + span, :]
    acc = b_ref[...]
    for k in range(CONV_KERNEL):
        off = CONV_HALO - CONV_KERNEL // 2 + k
        ph, base = off % SUBLANES, off - off % SUBLANES
        src = ext_ref[base:base + tc, :] if ph == 0 else sh_ref[ph, base:base + tc, :]
        acc = acc + w_ref[k:k + 1, :] * src
    mu = jnp.mean(acc, axis=-1, keepdims=True)
    cen = acc - mu
    var = jnp.mean(cen * cen, axis=-1, keepdims=True)
    z = cen * lax.rsqrt(var + EPS) * lg_ref[...] + lb_ref[...]
    o_ref[...] = (z * jax.nn.sigmoid(z)).astype(BF16)


def _convmod(ab3, dw_w, dw_b, cln_g, cln_b):
    bsz, s, _ = ab3.shape
    tc = SEQ_CHUNK
    n = s // tc
    hb = tc // CONV_HALO
    w2 = lambda b, j: (0, 0)
    ucol = 2 * LRU_WIDTH // CONV_WIDTH
    gcol = ucol + 1
    cur = lambda col: pl.BlockSpec((None, tc, CONV_WIDTH), lambda b, j: (b, j, col))
    prev = lambda col: pl.BlockSpec((None, CONV_HALO, CONV_WIDTH),
                                    lambda b, j: (b, jnp.maximum(j * hb - 1, 0), col))
    nxt = lambda col: pl.BlockSpec((None, CONV_HALO, CONV_WIDTH),
                                   lambda b, j: (b, jnp.minimum((j + 1) * hb, s // CONV_HALO - 1), col))
    return pl.pallas_call(
        functools.partial(_convmod_kernel, n_chunks=n),
        out_shape=jax.ShapeDtypeStruct((bsz, s, CONV_WIDTH), BF16),
        grid=(bsz, n),
        in_specs=[
            cur(ucol), cur(gcol), prev(ucol), prev(gcol), nxt(ucol), nxt(gcol),
            pl.BlockSpec((CONV_KERNEL, CONV_WIDTH), w2),
            pl.BlockSpec((1, CONV_WIDTH), w2),
            pl.BlockSpec((1, CONV_WIDTH), w2),
            pl.BlockSpec((1, CONV_WIDTH), w2),
        ],
        out_specs=pl.BlockSpec((None, tc, CONV_WIDTH), lambda b, j: (b, j, 0)),
        scratch_shapes=[pltpu.VMEM((tc + 2 * CONV_HALO, CONV_WIDTH), F32),
                        pltpu.VMEM((SUBLANES, tc + 2 * CONV_HALO - SUBLANES, CONV_WIDTH), F32)],
        compiler_params=_cparams("arbitrary", "arbitrary"),
        name="convmod",
    )(ab3, ab3, ab3, ab3, ab3, ab3, dw_w, dw_b, cln_g, cln_b)


def _attn_kernel(q_ref, k_ref, vt_ref, lv_ref, sg_ref, o_ref, st0_ref, st1_ref, m0_ref, m1_ref, *, lam_init):
    tq = q_ref.shape[0]
    g = pl.program_id(0)

    @pl.when(g == 0)
    def _():
        st1_ref[...] = jnp.zeros_like(st1_ref)
        m1_ref[...] = jnp.zeros_like(m1_ref)

    def step(st_new_ref, m_new_ref, st_old_ref, m_old_ref):
        q = q_ref[...]
        lane = lax.broadcasted_iota(jnp.int32, q.shape, 1)
        zero = jnp.zeros_like(q)
        qq = jnp.concatenate([jnp.where(lane < ATT_QKDIM, q, zero), jnp.where(lane >= ATT_QKDIM, q, zero)], axis=0)
        kc = KEY_CHUNK
        n_chunks = k_ref.shape[0] // kc
        ones = jnp.ones((ONES_ROWS, kc), BF16)
        m_old = m_old_ref[...]
        chunk = lambda c: slice(c * kc, (c + 1) * kc)

        def scores(c, m_new):
            st_c = lax.dot_general(k_ref[chunk(c), :], qq, (((1,), (1,)), ((), ())), preferred_element_type=F32)
            st_new_ref[chunk(c), :] = st_c
            m_c = jnp.max(st_c, axis=0, keepdims=True)
            return m_c if m_new is None else jnp.maximum(m_new, m_c)

        m_new = scores(0, None)
        ovt = None
        for c in range(n_chunks):
            e_c = jnp.exp(st_old_ref[chunk(c), :] - m_old).astype(BF16)
            if c + 1 < n_chunks:
                m_new = scores(c + 1, m_new)
            vt_c = jnp.concatenate([vt_ref[:, chunk(c)], ones], axis=0)
            o_c = jnp.dot(vt_c, e_c, preferred_element_type=F32)
            ovt = o_c if ovt is None else ovt + o_c
        m_new_ref[...] = m_new
        ovt = ovt[0:ATT_VDIM, :] / ovt[ATT_VDIM:ATT_VDIM + 1, :]
        lv = lv_ref[...]
        lam = (jnp.exp(jnp.sum(lv[0:1, :] * lv[1:2, :], axis=-1, keepdims=True))
               - jnp.exp(jnp.sum(lv[2:3, :] * lv[3:4, :], axis=-1, keepdims=True)) + lam_init)
        ot = ovt[:, 0:tq] - lam * ovt[:, tq:]
        ot = ot * lax.rsqrt(jnp.mean(ot * ot, axis=0, keepdims=True) + EPS) * sg_ref[...] * (1.0 - lam_init)
        o_ref[...] = ot.T.astype(BF16)

    @pl.when(g % 2 == 0)
    def _():
        step(st0_ref, m0_ref, st1_ref, m1_ref)

    @pl.when(g % 2 == 1)
    def _():
        step(st1_ref, m1_ref, st0_ref, m0_ref)


def _attn(qk3, vt3, lam_vec, subln_g, lam_init):
    bsz, s, _ = qk3.shape
    tq = min(MAX_Q_TILE, SCORE_TILE_BYTES // (2 * s * 4))
    n_q = s // tq
    n_tiles = bsz * ATT_HEADS * n_q
    kcol = ATT_WIDTH // LANES
    w2 = lambda g: (0, 0)
    cur = lambda g: jnp.minimum(g, n_tiles - 1)
    prv = lambda g: jnp.maximum(g - 1, 0)
    bat = lambda t: t // (ATT_HEADS * n_q)
    head = lambda t: (t // n_q) % ATT_HEADS
    qi = lambda t: t % n_q
    return pl.pallas_call(
        functools.partial(_attn_kernel, lam_init=lam_init),
        out_shape=jax.ShapeDtypeStruct((bsz, s, ATT_WIDTH), BF16),
        grid=(n_tiles + 1,),
        in_specs=[
            pl.BlockSpec((None, tq, LANES), lambda g: (bat(cur(g)), qi(cur(g)), head(cur(g)))),
            pl.BlockSpec((None, s, LANES), lambda g: (bat(cur(g)), 0, kcol + head(cur(g)))),
            pl.BlockSpec((None, ATT_VDIM, s), lambda g: (bat(prv(g)), head(prv(g)), 0)),
            pl.BlockSpec((4, ATT_QKDIM), w2),
            pl.BlockSpec((ATT_VDIM, 1), w2),
        ],
        out_specs=pl.BlockSpec((None, tq, ATT_VDIM), lambda g: (bat(prv(g)), qi(prv(g)), head(prv(g)))),
        scratch_shapes=[pltpu.VMEM((s, 2 * tq), F32), pltpu.VMEM((s, 2 * tq), F32),
                        pltpu.VMEM((1, 2 * tq), F32), pltpu.VMEM((1, 2 * tq), F32)],
        compiler_params=_cparams("arbitrary"),
        name="attn",
    )(qk3, qk3, vt3, lam_vec, subln_g.reshape(ATT_VDIM, 1))


def _first_argmax(vals):
    best = vals[0]
    idx = jnp.zeros(best.shape, jnp.int32)
    for i in range(1, len(vals)):
        better = vals[i] > best
        idx = jnp.where(better, i, idx)
        best = jnp.where(better, vals[i], best)
    return idx, best


def _out_proj_kernel(x_ref, ya_ref, yb_ref, yc_ref, w_ref, mod_ref, g_ref, rw_ref, rb_ref,
                     xo_ref, hp_ref, meta_ref, cnt_ref):
    tm = TOKEN_TILE
    y = jnp.dot(ya_ref[...], w_ref[0:LRU_WIDTH, :], preferred_element_type=F32)
    y = y + jnp.dot(yb_ref[...], w_ref[LRU_WIDTH:LRU_WIDTH + CONV_WIDTH, :], preferred_element_type=F32)
    y = y + jnp.dot(yc_ref[...], w_ref[LRU_WIDTH + CONV_WIDTH:, :], preferred_element_type=F32)
    x = x_ref[...] + mod_ref[2:3, :] * y
    xo_ref[...] = x
    h = _rms(x, g_ref[...]) * (1.0 + mod_ref[4:5, :]) + mod_ref[3:4, :]
    hp_ref[:, 0:D_MODEL] = h

    logits = lax.dot_general(rw_ref[...], h, (((1,), (1,)), ((), ())), precision=lax.Precision.HIGHEST,
                             preferred_element_type=F32) + rb_ref[...]
    gl = [logits[g:g + 1, :] for g in range(N_GROUPS)]
    g_idx, g_max = _first_argmax(gl)
    denom = gl[0] * 0.0
    for g in range(N_GROUPS):
        denom = denom + jnp.exp(gl[g] - g_max)
    g_w = 1.0 / denom
    fl = []
    for e in range(EXPERTS_PER_GROUP):
        sel = logits[N_GROUPS + e:N_GROUPS + e + 1, :]
        for g in range(1, N_GROUPS):
            r0 = N_GROUPS + g * EXPERTS_PER_GROUP + e
            sel = jnp.where(g_idx == g, logits[r0:r0 + 1, :], sel)
        fl.append(sel)
    i1, v1 = _first_argmax(fl)
    neg = jnp.full(v1.shape, -jnp.inf, F32)
    i2, v2 = _first_argmax([jnp.where(i1 == e, neg, fl[e]) for e in range(EXPERTS_PER_GROUP)])
    t = jnp.exp(v2 - v1)
    p1 = g_w / (1.0 + t)
    p2 = g_w * t / (1.0 + t)
    first_lo = i1 < i2
    lo = jnp.where(first_lo, i1, i2)
    hi = jnp.where(first_lo, i2, i1)
    pair = jnp.where(lo == 0, 0, jnp.where(lo == 1, 3, 5)) + hi - lo - 1
    bucket = g_idx * PAIRS_PER_GROUP + pair
    p_lo = jnp.where(first_lo, p1, p2)
    p_hi = jnp.where(first_lo, p2, p1)
    wrows = jnp.concatenate([p_lo, p_hi, jnp.zeros((LANES - 2, tm), F32)], axis=0)
    hp_ref[:, D_MODEL:] = wrows.T

    @pl.when(pl.program_id(0) == 0)
    def _():
        cnt_ref[...] = jnp.zeros_like(cnt_ref)

    onehot = lax.broadcasted_iota(jnp.int32, (BUCKET_ROWS, tm), 0) == bucket
    before = lax.broadcasted_iota(jnp.int32, (tm, tm), 0) < lax.broadcasted_iota(jnp.int32, (tm, tm), 1)
    prefix = jnp.dot(jnp.where(onehot, 1.0, 0.0).astype(BF16), jnp.where(before, 1.0, 0.0).astype(BF16),
                     preferred_element_type=F32)
    base = cnt_ref[:, 0:1]
    rank = jnp.sum(jnp.where(onehot, prefix + base, 0.0), axis=0, keepdims=True)
    cnt_ref[...] = cnt_ref[...] + jnp.sum(jnp.where(onehot, 1.0, 0.0), axis=1, keepdims=True)
    meta_ref[...] = jnp.concatenate([bucket.astype(F32), rank, jnp.zeros((SUBLANES - 2, tm), F32)], axis=0)


def _out_proj(x, ya, yb, yc, w_out_bf, mod, g2, rw_t, rb, s):
    t = x.shape[0]
    tm = TOKEN_TILE
    per_seq = s // tm
    row = lambda i: (i, 0)
    w2 = lambda i: (0, 0)
    return pl.pallas_call(
        _out_proj_kernel,
        out_shape=(
            jax.ShapeDtypeStruct((t, D_MODEL), F32),
            jax.ShapeDtypeStruct((t, PAYLOAD_COLS), F32),
            jax.ShapeDtypeStruct((SUBLANES, t), F32),
            jax.ShapeDtypeStruct((BUCKET_ROWS, LANES), F32),
        ),
        grid=(t // tm,),
        in_specs=[
            pl.BlockSpec((tm, D_MODEL), row),
            pl.BlockSpec((tm, LRU_WIDTH), row),
            pl.BlockSpec((tm, CONV_WIDTH), row),
            pl.BlockSpec((tm, ATT_WIDTH), row),
            pl.BlockSpec((D_MODEL, D_MODEL), w2),
            pl.BlockSpec((None, 6, D_MODEL), lambda i: (i // per_seq, 0, 0)),
            pl.BlockSpec((1, D_MODEL), w2),
            pl.BlockSpec((ROUTER_ROWS, D_MODEL), w2),
            pl.BlockSpec((ROUTER_ROWS, 1), w2),
        ],
        out_specs=(
            pl.BlockSpec((tm, D_MODEL), row),
            pl.BlockSpec((tm, PAYLOAD_COLS), row),
            pl.BlockSpec((SUBLANES, tm), lambda i: (0, i)),
            pl.BlockSpec((BUCKET_ROWS, LANES), w2),
        ),
        compiler_params=_cparams("arbitrary"),
        name="out_proj",
    )(x, ya, yb, yc, w_out_bf, mod, g2, rw_t, rb)


def _route_tables(meta, cnt, t):
    tm = TOKEN_TILE
    n_tiles = t // tm + N_BUCKETS
    counts = cnt[:N_BUCKETS, 0].astype(jnp.int32)
    tiles_per = (counts + tm - 1) // tm
    tile_end = jnp.cumsum(tiles_per)
    offs = (tile_end - tiles_per) * tm
    n_used = tile_end[-1:]
    bucket = meta[0].astype(jnp.int32)
    rank = meta[1].astype(jnp.int32)
    ids = jnp.arange(N_BUCKETS, dtype=jnp.int32)
    dest = rank + jnp.sum(jnp.where(bucket[:, None] == ids[None, :], offs[None, :], 0), axis=1)
    tile = jnp.minimum(jnp.arange(n_tiles, dtype=jnp.int32), n_used - 1)
    tb = jnp.sum((tile[:, None] >= tile_end[None, :]).astype(jnp.int32), axis=1)
    pair_lo = jnp.array([0, 0, 0, 1, 1, 2], jnp.int32)
    pair_hi = jnp.array([1, 2, 3, 2, 3, 3], jnp.int32)
    onehot_pair = (tb % PAIRS_PER_GROUP)[:, None] == jnp.arange(PAIRS_PER_GROUP, dtype=jnp.int32)[None, :]
    e_lo = (tb // PAIRS_PER_GROUP) * EXPERTS_PER_GROUP + jnp.sum(jnp.where(onehot_pair, pair_lo[None, :], 0), axis=1)
    e_hi = (tb // PAIRS_PER_GROUP) * EXPERTS_PER_GROUP + jnp.sum(jnp.where(onehot_pair, pair_hi[None, :], 0), axis=1)
    return dest, jnp.stack([e_lo, e_hi], axis=1).reshape(-1), n_used, n_tiles


def _row_copy(src_ref, src_row, dst_ref, dst_row, sem):
    return pltpu.make_async_copy(src_ref.at[pl.ds(src_row, 1), :], dst_ref.at[pl.ds(dst_row, 1), :], sem)


def _issue_rows(n_rows, start_row_copy):
    def pair(k, carry):
        start_row_copy(2 * k, 0)
        start_row_copy(2 * k + 1, 1)
        return carry

    lax.fori_loop(0, n_rows // 2, pair, 0, unroll=4)


def _dispatch_kernel(dest_ref, hp_ref, xs_in_ref, xs_ref, sem):
    del xs_in_ref
    tm = TOKEN_TILE
    base = pl.program_id(0) * tm

    _issue_rows(tm, lambda r, pri: _row_copy(hp_ref, r, xs_ref, dest_ref[base + r], sem).start(priority=pri))
    pltpu.make_async_copy(hp_ref, xs_ref.at[pl.ds(0, tm), :], sem).wait()


def _dispatch(dest, hp, n_rows):
    t = hp.shape[0]
    tm = TOKEN_TILE
    xs0 = jnp.zeros((n_rows, PAYLOAD_COLS), F32)
    return pl.pallas_call(
        _dispatch_kernel,
        out_shape=jax.ShapeDtypeStruct((n_rows, PAYLOAD_COLS), F32),
        grid_spec=pltpu.PrefetchScalarGridSpec(
            num_scalar_prefetch=1,
            grid=(t // tm,),
            in_specs=[
                pl.BlockSpec((tm, PAYLOAD_COLS), lambda i, dest: (i, 0)),
                pl.BlockSpec(memory_space=pl.ANY),
            ],
            out_specs=pl.BlockSpec(memory_space=pl.ANY),
            scratch_shapes=[pltpu.SemaphoreType.DMA],
        ),
        input_output_aliases={2: 0},
        compiler_params=_cparams("arbitrary"),
        name="dispatch",
    )(dest, hp, xs0)


def _experts_kernel(eid_ref, nused_ref, x_ref, w1a_ref, w3a_ref, w2a_ref, w1b_ref, w3b_ref, w2b_ref, o_ref,
                    w1a_bf, w3a_bf, w2a_bf, w1b_bf, w3b_bf, w2b_bf):
    i = pl.program_id(0)
    busy = i < nused_ref[0]
    prev = jnp.maximum(i - 1, 0)

    def refresh(slot, pairs):
        @pl.when(jnp.logical_or(i == 0, eid_ref[2 * i + slot] != eid_ref[2 * prev + slot]))
        def _():
            for w_ref, w_bf in pairs:
                w_bf[...] = w_ref[...].astype(BF16)

    refresh(0, ((w1a_ref, w1a_bf), (w3a_ref, w3a_bf), (w2a_ref, w2a_bf)))
    refresh(1, ((w1b_ref, w1b_bf), (w3b_ref, w3b_bf), (w2b_ref, w2b_bf)))

    @pl.when(jnp.logical_not(busy))
    def _():
        o_ref[...] = jnp.zeros_like(o_ref)

    @pl.when(busy)
    def _():
        x = x_ref[:, 0:D_MODEL].astype(BF16)

        def expert(w1_bf, w3_bf, w2_bf):
            a = jnp.dot(x, w1_bf[...], preferred_element_type=F32)
            he = (a * jax.nn.sigmoid(a)) * jnp.dot(x, w3_bf[...], preferred_element_type=F32)
            return jnp.dot(he.astype(BF16), w2_bf[...], preferred_element_type=F32)

        o_ref[...] = (x_ref[:, D_MODEL:D_MODEL + 1] * expert(w1a_bf, w3a_bf, w2a_bf)
                      + x_ref[:, D_MODEL + 1:D_MODEL + 2] * expert(w1b_bf, w3b_bf, w2b_bf))


def _experts(eids, n_used, xs, w1, w3, w2, layer):
    n_rows = xs.shape[0]
    tm = TOKEN_TILE
    rows = lambda i, eid, nu: (jnp.maximum(jnp.minimum(i, nu[0] - 1), 0), 0)
    wspec = lambda shape, slot: pl.BlockSpec((None, None) + shape,
                                             lambda i, eid, nu: (layer, eid[2 * i + slot], 0, 0))
    up, down = (D_MODEL, D_EXPERT), (D_EXPERT, D_MODEL)
    return pl.pallas_call(
        _experts_kernel,
        out_shape=jax.ShapeDtypeStruct((n_rows, D_MODEL), F32),
        grid_spec=pltpu.PrefetchScalarGridSpec(
            num_scalar_prefetch=2,
            grid=(n_rows // tm,),
            in_specs=[
                pl.BlockSpec((tm, PAYLOAD_COLS), rows),
                wspec(up, 0), wspec(up, 0), wspec(down, 0),
                wspec(up, 1), wspec(up, 1), wspec(down, 1),
            ],
            out_specs=pl.BlockSpec((tm, D_MODEL), lambda i, eid, nu: (i, 0)),
            scratch_shapes=[pltpu.VMEM(up, BF16), pltpu.VMEM(up, BF16), pltpu.VMEM(down, BF16),
                            pltpu.VMEM(up, BF16), pltpu.VMEM(up, BF16), pltpu.VMEM(down, BF16)],
        ),
        compiler_params=_cparams("arbitrary"),
        name="experts",
    )(eids, n_used, xs, w1, w3, w2, w1, w3, w2)


def _combine_kernel(dest_ref, x_ref, ys_ref, mod_ref, g_ref, o_ref, ybuf_ref, sem, *, final):
    tm = TOKEN_TILE
    base = pl.program_id(0) * tm

    _issue_rows(tm, lambda r, pri: _row_copy(ys_ref, dest_ref[base + r], ybuf_ref, r, sem).start(priority=pri))
    pltpu.make_async_copy(ys_ref.at[pl.ds(0, tm), :], ybuf_ref, sem).wait()
    x = x_ref[...] + mod_ref[5:6, :] * ybuf_ref[...]
    o_ref[...] = _rms(x, g_ref[...]) if final else x


def _combine(dest, x, ys, mod, final_g, s, final):
    t = x.shape[0]
    tm = TOKEN_TILE
    per_seq = s // tm
    return pl.pallas_call(
        functools.partial(_combine_kernel, final=final),
        out_shape=jax.ShapeDtypeStruct((t, D_MODEL), F32),
        grid_spec=pltpu.PrefetchScalarGridSpec(
            num_scalar_prefetch=1,
            grid=(t // tm,),
            in_specs=[
                pl.BlockSpec((tm, D_MODEL), lambda i, dest: (i, 0)),
                pl.BlockSpec(memory_space=pl.ANY),
                pl.BlockSpec((None, 6, D_MODEL), lambda i, dest: (i // per_seq, 0, 0)),
                pl.BlockSpec((1, D_MODEL), lambda i, dest: (0, 0)),
            ],
            out_specs=pl.BlockSpec((tm, D_MODEL), lambda i, dest: (i, 0)),
            scratch_shapes=[pltpu.VMEM((tm, D_MODEL), F32), pltpu.SemaphoreType.DMA],
        ),
        compiler_params=_cparams("arbitrary"),
        name="combine",
    )(dest, x, ys, mod, final_g)


def _block_diag(w):
    h, dh, _ = w.shape
    eye = jnp.eye(h, dtype=w.dtype)
    return (eye[:, None, :, None] * w[:, :, None, :]).reshape(h * dh, h * dh)


def _prep_layer(l, p):
    row = lambda a: a.reshape(1, -1)
    rw = jnp.concatenate([p["router_w1"][l], p["router_w2"][l].reshape(D_MODEL, N_EXPERTS)], axis=1)
    rw = jnp.pad(rw, ((0, 0), (0, ROUTER_ROWS - rw.shape[1])))
    rb = jnp.concatenate([p["router_b1"][l], p["router_b2"][l].reshape(N_EXPERTS)])
    rb = jnp.pad(rb, (0, ROUTER_ROWS - rb.shape[0]))
    return dict(
        norm1_g=row(p["norm1_g"][l]),
        norm2_g=row(p["norm2_g"][l]),
        w_in=p["w_in"][l].astype(BF16),
        w_out=p["w_out"][l].astype(BF16),
        conv_w=p["conv_w"][l],
        conv_b=row(p["conv_b"][l]),
        wa_bd=jnp.stack([_block_diag(p["lru_wa"][l, d]) for d in range(2)]).astype(BF16),
        wx_bd=jnp.stack([_block_diag(p["lru_wx"][l, d]) for d in range(2)]).astype(BF16),
        ba=p["lru_ba"][l].reshape(2, 1, LRU_WIDTH),
        bx=p["lru_bx"][l].reshape(2, 1, LRU_WIDTH),
        lam=p["lru_lambda"][l].reshape(2, 1, LRU_WIDTH),
        dw_w=p["dw_w"][l],
        dw_b=row(p["dw_b"][l]),
        cln_g=row(p["cln_g"][l]),
        cln_b=row(p["cln_b"][l]),
        lam_vec=p["lam_vec"][l],
        subln_g=row(p["subln_g"][l]),
        rw_t=rw.T,
        rb=rb.reshape(ROUTER_ROWS, 1),
    )


def _trunk(x, mods, layers, moe, final_g):
    bsz, s, _ = x.shape
    t = bsz * s
    tables = _rope_tables(s)
    x = x.reshape(t, D_MODEL)
    for l, w in enumerate(layers):
        mod = mods[l]
        ab, qk, vt = _in_proj(x, mod, w["norm1_g"], w["w_in"], tables, s)
        ab3 = ab.reshape(bsz, s, AB_COLS)
        ya = _lru(ab3, w["conv_w"], w["conv_b"], w["wa_bd"], w["ba"], w["wx_bd"], w["bx"], w["lam"])
        yb = _convmod(ab3, w["dw_w"], w["dw_b"], w["cln_g"], w["cln_b"])
        lam_init = 0.8 - 0.6 * math.exp(-0.3 * l)
        yc = _attn(qk.reshape(bsz, s, QK_COLS), vt, w["lam_vec"], w["subln_g"], lam_init)
        x, hp, meta, cnt = _out_proj(x, ya.reshape(t, LRU_WIDTH), yb.reshape(t, CONV_WIDTH),
                                     yc.reshape(t, ATT_WIDTH), w["w_out"], mod, w["norm2_g"], w["rw_t"], w["rb"], s)
        dest, eids, n_used, n_tiles = _route_tables(meta, cnt, t)
        xs = _dispatch(dest, hp, n_tiles * TOKEN_TILE)
        ys = _experts(eids, n_used, xs, moe["moe_w1"], moe["moe_w3"], moe["moe_w2"], l)
        x = _combine(dest, x, ys, mod, final_g, s, final=(l == DEPTH - 1))
    return x.reshape(bsz, s, D_MODEL)


def kernel(x_prompt, x_sample, c_prompt, c_sample, norm1_g, norm2_g, final_g, ada_w, ada_b, w_in, w_out, conv_w, conv_b, lru_wa, lru_ba, lru_wx, lru_bx, lru_lambda, dw_w, dw_b, cln_g, cln_b, lam_vec, subln_g, router_w1, router_b1, router_w2, router_b2, moe_w1, moe_w3, moe_w2):
    p = dict(norm1_g=norm1_g, norm2_g=norm2_g, w_in=w_in, w_out=w_out, conv_w=conv_w, conv_b=conv_b,
             lru_wa=lru_wa, lru_ba=lru_ba, lru_wx=lru_wx, lru_bx=lru_bx, lru_lambda=lru_lambda,
             dw_w=dw_w, dw_b=dw_b, cln_g=cln_g, cln_b=cln_b, lam_vec=lam_vec, subln_g=subln_g,
             router_w1=router_w1, router_b1=router_b1, router_w2=router_w2, router_b2=router_b2,
             moe_w1=moe_w1, moe_w3=moe_w3, moe_w2=moe_w2)
    layers = [_prep_layer(l, p) for l in range(DEPTH)]
    nb = c_prompt.shape[0]
    mods = _ada_mod(jnp.concatenate([c_prompt, c_sample], axis=0), ada_w, ada_b)
    fg = final_g.reshape(1, D_MODEL)
    moe = dict(moe_w1=moe_w1, moe_w3=moe_w3, moe_w2=moe_w2)
    y_prompt = _trunk(x_prompt, mods[:, :nb], layers, moe, fg)
    y_sample = _trunk(x_sample, mods[:, nb:], layers, moe, fg)
    return (y_prompt, y_sample)
```

```python
import functools
import math

import jax
import jax.numpy as jnp
from jax import lax
from jax.experimental import pallas as pl
from jax.experimental.pallas import tpu as pltpu

F32 = jnp.float32
BF16 = jnp.bfloat16

D_MODEL = 1024
DEPTH = 2
LRU_WIDTH = 256
LRU_HEADS = 4
LRU_CONV = 4
LRU_C = 8.0
CONV_WIDTH = 256
CONV_KERNEL = 31
ATT_WIDTH = 512
ATT_HEADS = 4
ATT_VDIM = 128
ATT_QKDIM = 64
ROPE_DIM = 16
ROPE_THETA = 500000.0
AB_COLS = 2 * LRU_WIDTH + 2 * CONV_WIDTH
QK_COLS = 2 * ATT_WIDTH
IN_COLS = AB_COLS + QK_COLS + ATT_WIDTH
N_GROUPS = 4
EXPERTS_PER_GROUP = 4
N_EXPERTS = 16
D_EXPERT = 512
EPS = 1e-6

LANES = 128
SUBLANES = 8
ROUTER_ROWS = 24
PAIRS_PER_GROUP = 6
N_BUCKETS = N_GROUPS * PAIRS_PER_GROUP
BUCKET_ROWS = 32
PAYLOAD_COLS = D_MODEL + LANES
VMEM_LIMIT = 56 * 1024 * 1024

TOKEN_TILE = 512
SEQ_CHUNK = 512
SCORE_TILE_BYTES = 16 * 1024 * 1024
MAX_Q_TILE = 512
ONES_ROWS = 16
KEY_CHUNK = 256
LRU_HALO = 8
CONV_HALO = 16


def _cparams(*sem):
    return pltpu.CompilerParams(dimension_semantics=sem, vmem_limit_bytes=VMEM_LIMIT)


def _rms(x, g):
    return x * lax.rsqrt(jnp.mean(x * x, axis=-1, keepdims=True) + EPS) * g


def _ada_kernel(c_ref, w_ref, b_ref, o_ref):
    c = c_ref[...]
    cs = (c * jax.nn.sigmoid(c)).astype(BF16)
    o_ref[...] = jnp.dot(cs, w_ref[...].astype(BF16), preferred_element_type=F32) + b_ref[...]


def _ada_mod(c, ada_w, ada_b):
    nb = c.shape[0]
    tn = 1536
    out = pl.pallas_call(
        _ada_kernel,
        out_shape=jax.ShapeDtypeStruct((DEPTH, nb, 6 * D_MODEL), F32),
        grid=(DEPTH, 6 * D_MODEL // tn),
        in_specs=[
            pl.BlockSpec((nb, D_MODEL), lambda l, j: (0, 0)),
            pl.BlockSpec((None, D_MODEL, tn), lambda l, j: (l, 0, j)),
            pl.BlockSpec((None, 1, tn), lambda l, j: (l, 0, j)),
        ],
        out_specs=pl.BlockSpec((None, nb, tn), lambda l, j: (l, 0, j)),
        compiler_params=_cparams("arbitrary", "arbitrary"),
        name="ada_mod",
    )(c, ada_w, ada_b.reshape(DEPTH, 1, 6 * D_MODEL))
    return out.reshape(DEPTH, nb, 6, D_MODEL)


def _in_proj_kernel(x_ref, mod_ref, g_ref, w_ref, cos_ref, sa_ref, sb_ref, ab_ref, qk_ref, vt_ref):
    x = x_ref[...]
    h = _rms(x, g_ref[...]) * (1.0 + mod_ref[1:2, :]) + mod_ref[0:1, :]
    hb = h.astype(BF16)
    ab_ref[...] = jnp.dot(hb, w_ref[:, 0:AB_COLS], preferred_element_type=F32)
    qk = jnp.dot(hb, w_ref[:, AB_COLS:AB_COLS + QK_COLS], preferred_element_type=F32)
    cos, sa, sb = cos_ref[...], sa_ref[...], sb_ref[...]
    for c in range(QK_COLS // LANES):
        blk = qk[:, c * LANES:(c + 1) * LANES]
        rot = (blk * cos + pltpu.roll(blk, ROPE_DIM // 2, 1) * sa
               + pltpu.roll(blk, LANES - ROPE_DIM // 2, 1) * sb)
        if c < ATT_WIDTH // LANES:
            rot = rot * (ATT_QKDIM ** -0.5)
        qk_ref[:, c * LANES:(c + 1) * LANES] = rot.astype(BF16)
    v = jnp.dot(hb, w_ref[:, AB_COLS + QK_COLS:], preferred_element_type=F32)
    vt_ref[...] = v.T.astype(BF16)


def _rope_tables(s):
    half = ROPE_DIM // 2
    inv = ROPE_THETA ** (-(jnp.arange(half, dtype=F32) * 2.0) / ROPE_DIM)
    ang = jnp.arange(s, dtype=jnp.int32).astype(F32)[:, None] * inv[None, :]
    cos, sin = jnp.cos(ang), jnp.sin(ang)
    rest = ATT_QKDIM - ROPE_DIM
    one = jnp.ones((s, rest), F32)
    zero = jnp.zeros((s, rest), F32)
    zh = jnp.zeros((s, half), F32)
    reps = LANES // ATT_QKDIM
    cos_t = jnp.tile(jnp.concatenate([cos, cos, one], axis=1), (1, reps))
    sa_t = jnp.tile(jnp.concatenate([zh, sin, zero], axis=1), (1, reps))
    sb_t = jnp.tile(jnp.concatenate([-sin, zh, zero], axis=1), (1, reps))
    return cos_t, sa_t, sb_t


def _in_proj(x, mod, g, w_in_bf, tables, s):
    t = x.shape[0]
    tm = TOKEN_TILE
    per_seq = s // tm
    row = lambda i: (i, 0)
    pos = lambda i: (i % per_seq, 0)
    return pl.pallas_call(
        _in_proj_kernel,
        out_shape=(
            jax.ShapeDtypeStruct((t, AB_COLS), F32),
            jax.ShapeDtypeStruct((t, QK_COLS), BF16),
            jax.ShapeDtypeStruct((t // s, ATT_WIDTH, s), BF16),
        ),
        grid=(t // tm,),
        in_specs=[
            pl.BlockSpec((tm, D_MODEL), row),
            pl.BlockSpec((None, 6, D_MODEL), lambda i: (i // per_seq, 0, 0)),
            pl.BlockSpec((1, D_MODEL), lambda i: (0, 0)),
            pl.BlockSpec((D_MODEL, IN_COLS), lambda i: (0, 0)),
            pl.BlockSpec((tm, LANES), pos),
            pl.BlockSpec((tm, LANES), pos),
            pl.BlockSpec((tm, LANES), pos),
        ],
        out_specs=(
            pl.BlockSpec((tm, AB_COLS), row),
            pl.BlockSpec((tm, QK_COLS), row),
            pl.BlockSpec((None, ATT_WIDTH, tm), lambda i: (i // per_seq, 0, i % per_seq)),
        ),
        compiler_params=_cparams("arbitrary"),
        name="in_proj",
    )(x, mod, g, w_in_bf, *tables)


def _neg_expm1(y):
    series = -y * (1.0 + y * (1.0 / 2.0) * (1.0 + y * (1.0 / 3.0) * (1.0 + y * (1.0 / 4.0) * (1.0 + y * (1.0 / 5.0)))))
    return jnp.where(y > -1.0 / 64.0, series, 1.0 - jnp.exp(y))


def _tile_scan(a, b, reverse):
    n = a.shape[0]
    a3 = a.reshape(n // SUBLANES, SUBLANES, a.shape[1])
    b3 = b.reshape(n // SUBLANES, SUBLANES, b.shape[1])
    row = lax.broadcasted_iota(jnp.int32, a3.shape, 1)
    for d in (1, 2, 4):
        shift = SUBLANES - d if reverse else d
        a_s = pltpu.roll(a3, shift, 1)
        b_s = pltpu.roll(b3, shift, 1)
        m = (row < SUBLANES - d) if reverse else (row >= d)
        b3 = jnp.where(m, b3 + a3 * b_s, b3)
        a3 = jnp.where(m, a3 * a_s, a3)
    return a3.reshape(a.shape), b3.reshape(b.shape)


def _lru_kernel(cur_ref, prev_ref, next_ref, ga_ref, cw_ref, cb_ref, wa_ref, ba_ref, wx_ref, bx_ref,
                lam_ref, o_ref, hf_ref, ext_ref, a_ref, b_ref, h_ref, carry_ref, *, n_chunks):
    tc = SEQ_CHUNK
    p = pl.program_id(1)
    j = pl.program_id(2)
    jx = j + p * (n_chunks - 1 - 2 * j)

    @pl.when(j == 0)
    def _():
        carry_ref[...] = jnp.zeros_like(carry_ref)

    zero_halo = jnp.zeros((LRU_HALO, LRU_WIDTH), F32)
    ext_ref[0:LRU_HALO, :] = jnp.where(jx == 0, zero_halo, prev_ref[...])
    ext_ref[LRU_HALO:LRU_HALO + tc, :] = cur_ref[...]
    ext_ref[LRU_HALO + tc:, :] = jnp.where(jx == n_chunks - 1, zero_halo, next_ref[...])
    xc = cb_ref[...]
    for k in range(LRU_CONV):
        off = LRU_HALO - 2 + k
        xc = xc + cw_ref[k:k + 1, :] * ext_ref[off:off + tc, :]
    xb = xc.astype(BF16)
    r = jax.nn.sigmoid(jnp.dot(xb, wa_ref[...], preferred_element_type=F32) + ba_ref[...])
    gate_i = jax.nn.sigmoid(jnp.dot(xb, wx_ref[...], preferred_element_type=F32) + bx_ref[...])
    log_a = (-LRU_C) * r * jax.nn.softplus(-lam_ref[...])
    a = jnp.exp(log_a)
    b = jnp.sqrt(_neg_expm1(2.0 * log_a)) * (gate_i * xc)

    n_tiles = tc // SUBLANES

    def run(reverse):
        a_s, b_s = _tile_scan(a, b, reverse)
        a_ref[...] = a_s
        b_ref[...] = b_s

        def body(i, hc):
            ti = (n_tiles - 1 - i) if reverse else i
            off = pl.multiple_of(ti * SUBLANES, SUBLANES)
            h = b_ref[pl.ds(off, SUBLANES), :] + a_ref[pl.ds(off, SUBLANES), :] * hc
            h_ref[pl.ds(off, SUBLANES), :] = h
            edge = h[0:1, :] if reverse else h[SUBLANES - 1:SUBLANES, :]
            return jnp.broadcast_to(edge, (SUBLANES, LRU_WIDTH))

        carry_ref[...] = lax.fori_loop(0, n_tiles, body, carry_ref[...], unroll=8)

    row0 = pl.multiple_of(jx * tc, tc)

    @pl.when(p == 0)
    def _():
        run(False)
        hf_ref[pl.ds(row0, tc), :] = h_ref[...]

    @pl.when(p == 1)
    def _():
        run(True)
        hsum = hf_ref[pl.ds(row0, tc), :] + h_ref[...]
        o_ref[...] = (jax.nn.gelu(ga_ref[...]) * hsum).astype(BF16)


def _lru(ab3, conv_w, conv_b, wa_bd, ba, wx_bd, bx, lam):
    bsz, s, _ = ab3.shape
    tc = SEQ_CHUNK
    n = s // tc
    hb = tc // LRU_HALO
    jx = lambda p, j: j + p * (n - 1 - 2 * j)
    jg = lambda p, j: n - 1 - p * j
    w2 = lambda b, p, j: (0, 0)
    wdir = lambda b, p, j: (p, 0, 0)
    return pl.pallas_call(
        functools.partial(_lru_kernel, n_chunks=n),
        out_shape=jax.ShapeDtypeStruct((bsz, s, LRU_WIDTH), BF16),
        grid=(bsz, 2, n),
        in_specs=[
            pl.BlockSpec((None, tc, LRU_WIDTH), lambda b, p, j: (b, jx(p, j), 0)),
            pl.BlockSpec((None, LRU_HALO, LRU_WIDTH),
                         lambda b, p, j: (b, jnp.maximum(jx(p, j) * hb - 1, 0), 0)),
            pl.BlockSpec((None, LRU_HALO, LRU_WIDTH),
                         lambda b, p, j: (b, jnp.minimum((jx(p, j) + 1) * hb, s // LRU_HALO - 1), 0)),
            pl.BlockSpec((None, tc, LRU_WIDTH), lambda b, p, j: (b, jg(p, j), 1)),
            pl.BlockSpec((LRU_CONV, LRU_WIDTH), w2),
            pl.BlockSpec((1, LRU_WIDTH), w2),
            pl.BlockSpec((None, LRU_WIDTH, LRU_WIDTH), wdir),
            pl.BlockSpec((None, 1, LRU_WIDTH), wdir),
            pl.BlockSpec((None, LRU_WIDTH, LRU_WIDTH), wdir),
            pl.BlockSpec((None, 1, LRU_WIDTH), wdir),
            pl.BlockSpec((None, 1, LRU_WIDTH), wdir),
        ],
        out_specs=pl.BlockSpec((None, tc, LRU_WIDTH), lambda b, p, j: (b, jg(p, j), 0)),
        scratch_shapes=[
            pltpu.VMEM((s, LRU_WIDTH), F32),
            pltpu.VMEM((tc + 2 * LRU_HALO, LRU_WIDTH), F32),
            pltpu.VMEM((tc, LRU_WIDTH), F32),
            pltpu.VMEM((tc, LRU_WIDTH), F32),
            pltpu.VMEM((tc, LRU_WIDTH), F32),
            pltpu.VMEM((SUBLANES, LRU_WIDTH), F32),
        ],
        compiler_params=_cparams("arbitrary", "arbitrary", "arbitrary"),
        name="lru",
    )(ab3, ab3, ab3, ab3, conv_w, conv_b, wa_bd, ba, wx_bd, bx, lam)


def _convmod_kernel(u_ref, g_ref, up_ref, gp_ref, un_ref, gn_ref, w_ref, b_ref, lg_ref, lb_ref,
                    o_ref, ext_ref, sh_ref, *, n_chunks):
    tc = SEQ_CHUNK
    j = pl.program_id(1)
    zero_halo = jnp.zeros((CONV_HALO, CONV_WIDTH), F32)
    glu = lambda u, g: u * jax.nn.sigmoid(g)
    ext_ref[0:CONV_HALO, :] = jnp.where(j == 0, zero_halo, glu(up_ref[...], gp_ref[...]))
    ext_ref[CONV_HALO:CONV_HALO + tc, :] = glu(u_ref[...], g_ref[...])
    ext_ref[CONV_HALO + tc:, :] = jnp.where(j == n_chunks - 1, zero_halo, glu(un_ref[...], gn_ref[...]))
    span = sh_ref.shape[1]
    for ph in range(1, SUBLANES):
        sh_ref[ph] = ext_ref[ph:ph + span, :]
    acc = b_ref[...]
    for k in range(CONV_KERNEL):
        off = CONV_HALO - CONV_KERNEL // 2 + k
        ph, base = off % SUBLANES, off - off % SUBLANES
        src = ext_ref[base:base + tc, :] if ph == 0 else sh_ref[ph, base:base + tc, :]
        acc = acc + w_ref[k:k + 1, :] * src
    mu = jnp.mean(acc, axis=-1, keepdims=True)
    cen = acc - mu
    var = jnp.mean(cen * cen, axis=-1, keepdims=True)
    z = cen * lax.rsqrt(var + EPS) * lg_ref[...] + lb_ref[...]
    o_ref[...] = (z * jax.nn.sigmoid(z)).astype(BF16)


def _convmod(ab3, dw_w, dw_b, cln_g, cln_b):
    bsz, s, _ = ab3.shape
    tc = SEQ_CHUNK
    n = s // tc
    hb = tc // CONV_HALO
    w2 = lambda b, j: (0, 0)
    ucol = 2 * LRU_WIDTH // CONV_WIDTH
    gcol = ucol + 1
    cur = lambda col: pl.BlockSpec((None, tc, CONV_WIDTH), lambda b, j: (b, j, col))
    prev = lambda col: pl.BlockSpec((None, CONV_HALO, CONV_WIDTH),
                                    lambda b, j: (b, jnp.maximum(j * hb - 1, 0), col))
    nxt = lambda col: pl.BlockSpec((None, CONV_HALO, CONV_WIDTH),
                                   lambda b, j: (b, jnp.minimum((j + 1) * hb, s // CONV_HALO - 1), col))
    return pl.pallas_call(
        functools.partial(_convmod_kernel, n_chunks=n),
        out_shape=jax.ShapeDtypeStruct((bsz, s, CONV_WIDTH), BF16),
        grid=(bsz, n),
        in_specs=[
            cur(ucol), cur(gcol), prev(ucol), prev(gcol), nxt(ucol), nxt(gcol),
            pl.BlockSpec((CONV_KERNEL, CONV_WIDTH), w2),
            pl.BlockSpec((1, CONV_WIDTH), w2),
            pl.BlockSpec((1, CONV_WIDTH), w2),
            pl.BlockSpec((1, CONV_WIDTH), w2),
        ],
        out_specs=pl.BlockSpec((None, tc, CONV_WIDTH), lambda b, j: (b, j, 0)),
        scratch_shapes=[pltpu.VMEM((tc + 2 * CONV_HALO, CONV_WIDTH), F32),
                        pltpu.VMEM((SUBLANES, tc + 2 * CONV_HALO - SUBLANES, CONV_WIDTH), F32)],
        compiler_params=_cparams("arbitrary", "arbitrary"),
        name="convmod",
    )(ab3, ab3, ab3, ab3, ab3, ab3, dw_w, dw_b, cln_g, cln_b)


def _attn_kernel(q_ref, k_ref, vt_ref, lv_ref, sg_ref, o_ref, st0_ref, st1_ref, m0_ref, m1_ref, *, lam_init):
    tq = q_ref.shape[0]
    g = pl.program_id(0)

    @pl.when(g == 0)
    def _():
        st1_ref[...] = jnp.zeros_like(st1_ref)
        m1_ref[...] = jnp.zeros_like(m1_ref)

    def step(st_new_ref, m_new_ref, st_old_ref, m_old_ref):
        q = q_ref[...]
        lane = lax.broadcasted_iota(jnp.int32, q.shape, 1)
        zero = jnp.zeros_like(q)
        qq = jnp.concatenate([jnp.where(lane < ATT_QKDIM, q, zero), jnp.where(lane >= ATT_QKDIM, q, zero)], axis=0)
        kc = KEY_CHUNK
        n_chunks = k_ref.shape[0] // kc
        ones = jnp.ones((ONES_ROWS, kc), BF16)
        m_old = m_old_ref[...]
        chunk = lambda c: slice(c * kc, (c + 1) * kc)

        def scores(c, m_new):
            st_c = lax.dot_general(k_ref[chunk(c), :], qq, (((1,), (1,)), ((), ())), preferred_element_type=F32)
            st_new_ref[chunk(c), :] = st_c
            m_c = jnp.max(st_c, axis=0, keepdims=True)
            return m_c if m_new is None else jnp.maximum(m_new, m_c)

        m_new = scores(0, None)
        ovt = None
        for c in range(n_chunks):
            e_c = jnp.exp(st_old_ref[chunk(c), :] - m_old).astype(BF16)
            if c + 1 < n_chunks:
                m_new = scores(c + 1, m_new)
            vt_c = jnp.concatenate([vt_ref[:, chunk(c)], ones], axis=0)
            o_c = jnp.dot(vt_c, e_c, preferred_element_type=F32)
            ovt = o_c if ovt is None else ovt + o_c
        m_new_ref[...] = m_new
        ovt = ovt[0:ATT_VDIM, :] / ovt[ATT_VDIM:ATT_VDIM + 1, :]
        lv = lv_ref[...]
        lam = (jnp.exp(jnp.sum(lv[0:1, :] * lv[1:2, :], axis=-1, keepdims=True))
               - jnp.exp(jnp.sum(lv[2:3, :] * lv[3:4, :], axis=-1, keepdims=True)) + lam_init)
        ot = ovt[:, 0:tq] - lam * ovt[:, tq:]
        ot = ot * lax.rsqrt(jnp.mean(ot * ot, axis=0, keepdims=True) + EPS) * sg_ref[...] * (1.0 - lam_init)
        o_ref[...] = ot.T.astype(BF16)

    @pl.when(g % 2 == 0)
    def _():
        step(st0_ref, m0_ref, st1_ref, m1_ref)

    @pl.when(g % 2 == 1)
    def _():
        step(st1_ref, m1_ref, st0_ref, m0_ref)


def _attn(qk3, vt3, lam_vec, subln_g, lam_init):
    bsz, s, _ = qk3.shape
    tq = min(MAX_Q_TILE, SCORE_TILE_BYTES // (2 * s * 4))
    n_q = s // tq
    n_tiles = bsz * ATT_HEADS * n_q
    kcol = ATT_WIDTH // LANES
    w2 = lambda g: (0, 0)
    cur = lambda g: jnp.minimum(g, n_tiles - 1)
    prv = lambda g: jnp.maximum(g - 1, 0)
    bat = lambda t: t // (ATT_HEADS * n_q)
    head = lambda t: (t // n_q) % ATT_HEADS
    qi = lambda t: t % n_q
    return pl.pallas_call(
        functools.partial(_attn_kernel, lam_init=lam_init),
        out_shape=jax.ShapeDtypeStruct((bsz, s, ATT_WIDTH), BF16),
        grid=(n_tiles + 1,),
        in_specs=[
            pl.BlockSpec((None, tq, LANES), lambda g: (bat(cur(g)), qi(cur(g)), head(cur(g)))),
            pl.BlockSpec((None, s, LANES), lambda g: (bat(cur(g)), 0, kcol + head(cur(g)))),
            pl.BlockSpec((None, ATT_VDIM, s), lambda g: (bat(prv(g)), head(prv(g)), 0)),
            pl.BlockSpec((4, ATT_QKDIM), w2),
            pl.BlockSpec((ATT_VDIM, 1), w2),
        ],
        out_specs=pl.BlockSpec((None, tq, ATT_VDIM), lambda g: (bat(prv(g)), qi(prv(g)), head(prv(g)))),
        scratch_shapes=[pltpu.VMEM((s, 2 * tq), F32), pltpu.VMEM((s, 2 * tq), F32),
                        pltpu.VMEM((1, 2 * tq), F32), pltpu.VMEM((1, 2 * tq), F32)],
        compiler_params=_cparams("arbitrary"),
        name="attn",
    )(qk3, qk3, vt3, lam_vec, subln_g.reshape(ATT_VDIM, 1))


def _first_argmax(vals):
    best = vals[0]
    idx = jnp.zeros(best.shape, jnp.int32)
    for i in range(1, len(vals)):
        better = vals[i] > best
        idx = jnp.where(better, i, idx)
        best = jnp.where(better, vals[i], best)
    return idx, best


def _out_proj_kernel(x_ref, ya_ref, yb_ref, yc_ref, w_ref, mod_ref, g_ref, rw_ref, rb_ref,
                     xo_ref, hp_ref, meta_ref, cnt_ref):
    tm = TOKEN_TILE
    y = jnp.dot(ya_ref[...], w_ref[0:LRU_WIDTH, :], preferred_element_type=F32)
    y = y + jnp.dot(yb_ref[...], w_ref[LRU_WIDTH:LRU_WIDTH + CONV_WIDTH, :], preferred_element_type=F32)
    y = y + jnp.dot(yc_ref[...], w_ref[LRU_WIDTH + CONV_WIDTH:, :], preferred_element_type=F32)
    x = x_ref[...] + mod_ref[2:3, :] * y
    xo_ref[...] = x
    h = _rms(x, g_ref[...]) * (1.0 + mod_ref[4:5, :]) + mod_ref[3:4, :]
    hp_ref[:, 0:D_MODEL] = h

    logits = lax.dot_general(rw_ref[...], h, (((1,), (1,)), ((), ())), precision=lax.Precision.HIGHEST,
                             preferred_element_type=F32) + rb_ref[...]
    gl = [logits[g:g + 1, :] for g in range(N_GROUPS)]
    g_idx, g_max = _first_argmax(gl)
    denom = gl[0] * 0.0
    for g in range(N_GROUPS):
        denom = denom + jnp.exp(gl[g] - g_max)
    g_w = 1.0 / denom
    fl = []
    for e in range(EXPERTS_PER_GROUP):
        sel = logits[N_GROUPS + e:N_GROUPS + e + 1, :]
        for g in range(1, N_GROUPS):
            r0 = N_GROUPS + g * EXPERTS_PER_GROUP + e
            sel = jnp.where(g_idx == g, logits[r0:r0 + 1, :], sel)
        fl.append(sel)
    i1, v1 = _first_argmax(fl)
    neg = jnp.full(v1.shape, -jnp.inf, F32)
    i2, v2 = _first_argmax([jnp.where(i1 == e, neg, fl[e]) for e in range(EXPERTS_PER_GROUP)])
    t = jnp.exp(v2 - v1)
    p1 = g_w / (1.0 + t)
    p2 = g_w * t / (1.0 + t)
    first_lo = i1 < i2
    lo = jnp.where(first_lo, i1, i2)
    hi = jnp.where(first_lo, i2, i1)
    pair = jnp.where(lo == 0, 0, jnp.where(lo == 1, 3, 5)) + hi - lo - 1
    bucket = g_idx * PAIRS_PER_GROUP + pair
    p_lo = jnp.where(first_lo, p1, p2)
    p_hi = jnp.where(first_lo, p2, p1)
    wrows = jnp.concatenate([p_lo, p_hi, jnp.zeros((LANES - 2, tm), F32)], axis=0)
    hp_ref[:, D_MODEL:] = wrows.T

    @pl.when(pl.program_id(0) == 0)
    def _():
        cnt_ref[...] = jnp.zeros_like(cnt_ref)

    onehot = lax.broadcasted_iota(jnp.int32, (BUCKET_ROWS, tm), 0) == bucket
    before = lax.broadcasted_iota(jnp.int32, (tm, tm), 0) < lax.broadcasted_iota(jnp.int32, (tm, tm), 1)
    prefix = jnp.dot(jnp.where(onehot, 1.0, 0.0).astype(BF16), jnp.where(before, 1.0, 0.0).astype(BF16),
                     preferred_element_type=F32)
    base = cnt_ref[:, 0:1]
    rank = jnp.sum(jnp.where(onehot, prefix + base, 0.0), axis=0, keepdims=True)
    cnt_ref[...] = cnt_ref[...] + jnp.sum(jnp.where(onehot, 1.0, 0.0), axis=1, keepdims=True)
    meta_ref[...] = jnp.concatenate([bucket.astype(F32), rank, jnp.zeros((SUBLANES - 2, tm), F32)], axis=0)


def _out_proj(x, ya, yb, yc, w_out_bf, mod, g2, rw_t, rb, s):
    t = x.shape[0]
    tm = TOKEN_TILE
    per_seq = s // tm
    row = lambda i: (i, 0)
    w2 = lambda i: (0, 0)
    return pl.pallas_call(
        _out_proj_kernel,
        out_shape=(
            jax.ShapeDtypeStruct((t, D_MODEL), F32),
            jax.ShapeDtypeStruct((t, PAYLOAD_COLS), F32),
            jax.ShapeDtypeStruct((SUBLANES, t), F32),
            jax.ShapeDtypeStruct((BUCKET_ROWS, LANES), F32),
        ),
        grid=(t // tm,),
        in_specs=[
            pl.BlockSpec((tm, D_MODEL), row),
            pl.BlockSpec((tm, LRU_WIDTH), row),
            pl.BlockSpec((tm, CONV_WIDTH), row),
            pl.BlockSpec((tm, ATT_WIDTH), row),
            pl.BlockSpec((D_MODEL, D_MODEL), w2),
            pl.BlockSpec((None, 6, D_MODEL), lambda i: (i // per_seq, 0, 0)),
            pl.BlockSpec((1, D_MODEL), w2),
            pl.BlockSpec((ROUTER_ROWS, D_MODEL), w2),
            pl.BlockSpec((ROUTER_ROWS, 1), w2),
        ],
        out_specs=(
            pl.BlockSpec((tm, D_MODEL), row),
            pl.BlockSpec((tm, PAYLOAD_COLS), row),
            pl.BlockSpec((SUBLANES, tm), lambda i: (0, i)),
            pl.BlockSpec((BUCKET_ROWS, LANES), w2),
        ),
        compiler_params=_cparams("arbitrary"),
        name="out_proj",
    )(x, ya, yb, yc, w_out_bf, mod, g2, rw_t, rb)


def _route_tables(meta, cnt, t):
    tm = TOKEN_TILE
    n_tiles = t // tm + N_BUCKETS
    counts = cnt[:N_BUCKETS, 0].astype(jnp.int32)
    tiles_per = (counts + tm - 1) // tm
    tile_end = jnp.cumsum(tiles_per)
    offs = (tile_end - tiles_per) * tm
    n_used = tile_end[-1:]
    bucket = meta[0].astype(jnp.int32)
    rank = meta[1].astype(jnp.int32)
    ids = jnp.arange(N_BUCKETS, dtype=jnp.int32)
    dest = rank + jnp.sum(jnp.where(bucket[:, None] == ids[None, :], offs[None, :], 0), axis=1)
    tile = jnp.minimum(jnp.arange(n_tiles, dtype=jnp.int32), n_used - 1)
    tb = jnp.sum((tile[:, None] >= tile_end[None, :]).astype(jnp.int32), axis=1)
    pair_lo = jnp.array([0, 0, 0, 1, 1, 2], jnp.int32)
    pair_hi = jnp.array([1, 2, 3, 2, 3, 3], jnp.int32)
    onehot_pair = (tb % PAIRS_PER_GROUP)[:, None] == jnp.arange(PAIRS_PER_GROUP, dtype=jnp.int32)[None, :]
    e_lo = (tb // PAIRS_PER_GROUP) * EXPERTS_PER_GROUP + jnp.sum(jnp.where(onehot_pair, pair_lo[None, :], 0), axis=1)
    e_hi = (tb // PAIRS_PER_GROUP) * EXPERTS_PER_GROUP + jnp.sum(jnp.where(onehot_pair, pair_hi[None, :], 0), axis=1)
    return dest, jnp.stack([e_lo, e_hi], axis=1).reshape(-1), n_used, n_tiles


def _row_copy(src_ref, src_row, dst_ref, dst_row, sem):
    return pltpu.make_async_copy(src_ref.at[pl.ds(src_row, 1), :], dst_ref.at[pl.ds(dst_row, 1), :], sem)


def _issue_rows(n_rows, start_row_copy):
    for r in range(n_rows):
        start_row_copy(r, r % 2)


def _dispatch_kernel(dest_ref, hp_ref, xs_in_ref, xs_ref, sem):
    del xs_in_ref
    tm = TOKEN_TILE
    base = pl.program_id(0) * tm

    _issue_rows(tm, lambda r, pri: _row_copy(hp_ref, r, xs_ref, dest_ref[base + r], sem).start(priority=pri))
    pltpu.make_async_copy(hp_ref, xs_ref.at[pl.ds(0, tm), :], sem).wait()


def _dispatch(dest, hp, n_rows):
    t = hp.shape[0]
    tm = TOKEN_TILE
    xs0 = jnp.zeros((n_rows, PAYLOAD_COLS), F32)
    return pl.pallas_call(
        _dispatch_kernel,
        out_shape=jax.ShapeDtypeStruct((n_rows, PAYLOAD_COLS), F32),
        grid_spec=pltpu.PrefetchScalarGridSpec(
            num_scalar_prefetch=1,
            grid=(t // tm,),
            in_specs=[
                pl.BlockSpec((tm, PAYLOAD_COLS), lambda i, dest: (i, 0)),
                pl.BlockSpec(memory_space=pl.ANY),
            ],
            out_specs=pl.BlockSpec(memory_space=pl.ANY),
            scratch_shapes=[pltpu.SemaphoreType.DMA],
        ),
        input_output_aliases={2: 0},
        compiler_params=_cparams("arbitrary"),
        name="dispatch",
    )(dest, hp, xs0)


def _experts_kernel(eid_ref, nused_ref, x_ref, w1a_ref, w3a_ref, w2a_ref, w1b_ref, w3b_ref, w2b_ref, o_ref,
                    w1a_bf, w3a_bf, w2a_bf, w1b_bf, w3b_bf, w2b_bf):
    i = pl.program_id(0)
    busy = i < nused_ref[0]
    prev = jnp.maximum(i - 1, 0)

    def refresh(slot, pairs):
        @pl.when(jnp.logical_or(i == 0, eid_ref[2 * i + slot] != eid_ref[2 * prev + slot]))
        def _():
            for w_ref, w_bf in pairs:
                w_bf[...] = w_ref[...].astype(BF16)

    refresh(0, ((w1a_ref, w1a_bf), (w3a_ref, w3a_bf), (w2a_ref, w2a_bf)))
    refresh(1, ((w1b_ref, w1b_bf), (w3b_ref, w3b_bf), (w2b_ref, w2b_bf)))

    @pl.when(jnp.logical_not(busy))
    def _():
        o_ref[...] = jnp.zeros_like(o_ref)

    @pl.when(busy)
    def _():
        x = x_ref[:, 0:D_MODEL].astype(BF16)

        def expert(w1_bf, w3_bf, w2_bf):
            a = jnp.dot(x, w1_bf[...], preferred_element_type=F32)
            he = (a * jax.nn.sigmoid(a)) * jnp.dot(x, w3_bf[...], preferred_element_type=F32)
            return jnp.dot(he.astype(BF16), w2_bf[...], preferred_element_type=F32)

        o_ref[...] = (x_ref[:, D_MODEL:D_MODEL + 1] * expert(w1a_bf, w3a_bf, w2a_bf)
                      + x_ref[:, D_MODEL + 1:D_MODEL + 2] * expert(w1b_bf, w3b_bf, w2b_bf))


def _experts(eids, n_used, xs, w1, w3, w2, layer):
    n_rows = xs.shape[0]
    tm = TOKEN_TILE
    rows = lambda i, eid, nu: (jnp.maximum(jnp.minimum(i, nu[0] - 1), 0), 0)
    wspec = lambda shape, slot: pl.BlockSpec((None, None) + shape,
                                             lambda i, eid, nu: (layer, eid[2 * i + slot], 0, 0))
    up, down = (D_MODEL, D_EXPERT), (D_EXPERT, D_MODEL)
    return pl.pallas_call(
        _experts_kernel,
        out_shape=jax.ShapeDtypeStruct((n_rows, D_MODEL), F32),
        grid_spec=pltpu.PrefetchScalarGridSpec(
            num_scalar_prefetch=2,
            grid=(n_rows // tm,),
            in_specs=[
                pl.BlockSpec((tm, PAYLOAD_COLS), rows),
                wspec(up, 0), wspec(up, 0), wspec(down, 0),
                wspec(up, 1), wspec(up, 1), wspec(down, 1),
            ],
            out_specs=pl.BlockSpec((tm, D_MODEL), lambda i, eid, nu: (i, 0)),
            scratch_shapes=[pltpu.VMEM(up, BF16), pltpu.VMEM(up, BF16), pltpu.VMEM(down, BF16),
                            pltpu.VMEM(up, BF16), pltpu.VMEM(up, BF16), pltpu.VMEM(down, BF16)],
        ),
        compiler_params=_cparams("arbitrary"),
        name="experts",
    )(eids, n_used, xs, w1, w3, w2, w1, w3, w2)


def _combine_kernel(dest_ref, x_ref, ys_ref, mod_ref, g_ref, o_ref, ybuf_ref, sem, *, final):
    tm = TOKEN_TILE
    base = pl.program_id(0) * tm

    _issue_rows(tm, lambda r, pri: _row_copy(ys_ref, dest_ref[base + r], ybuf_ref, r, sem).start(priority=pri))
    pltpu.make_async_copy(ys_ref.at[pl.ds(0, tm), :], ybuf_ref, sem).wait()
    x = x_ref[...] + mod_ref[5:6, :] * ybuf_ref[...]
    o_ref[...] = _rms(x, g_ref[...]) if final else x


def _combine(dest, x, ys, mod, final_g, s, final):
    t = x.shape[0]
    tm = TOKEN_TILE
    per_seq = s // tm
    return pl.pallas_call(
        functools.partial(_combine_kernel, final=final),
        out_shape=jax.ShapeDtypeStruct((t, D_MODEL), F32),
        grid_spec=pltpu.PrefetchScalarGridSpec(
            num_scalar_prefetch=1,
            grid=(t // tm,),
            in_specs=[
                pl.BlockSpec((tm, D_MODEL), lambda i, dest: (i, 0)),
                pl.BlockSpec(memory_space=pl.ANY),
                pl.BlockSpec((None, 6, D_MODEL), lambda i, dest: (i // per_seq, 0, 0)),
                pl.BlockSpec((1, D_MODEL), lambda i, dest: (0, 0)),
            ],
            out_specs=pl.BlockSpec((tm, D_MODEL), lambda i, dest: (i, 0)),
            scratch_shapes=[pltpu.VMEM((tm, D_MODEL), F32), pltpu.SemaphoreType.DMA],
        ),
        compiler_params=_cparams("arbitrary"),
        name="combine",
    )(dest, x, ys, mod, final_g)


def _block_diag(w):
    h, dh, _ = w.shape
    eye = jnp.eye(h, dtype=w.dtype)
    return (eye[:, None, :, None] * w[:, :, None, :]).reshape(h * dh, h * dh)


def _prep_layer(l, p):
    row = lambda a: a.reshape(1, -1)
    rw = jnp.concatenate([p["router_w1"][l], p["router_w2"][l].reshape(D_MODEL, N_EXPERTS)], axis=1)
    rw = jnp.pad(rw, ((0, 0), (0, ROUTER_ROWS - rw.shape[1])))
    rb = jnp.concatenate([p["router_b1"][l], p["router_b2"][l].reshape(N_EXPERTS)])
    rb = jnp.pad(rb, (0, ROUTER_ROWS - rb.shape[0]))
    return dict(
        norm1_g=row(p["norm1_g"][l]),
        norm2_g=row(p["norm2_g"][l]),
        w_in=p["w_in"][l].astype(BF16),
        w_out=p["w_out"][l].astype(BF16),
        conv_w=p["conv_w"][l],
        conv_b=row(p["conv_b"][l]),
        wa_bd=jnp.stack([_block_diag(p["lru_wa"][l, d]) for d in range(2)]).astype(BF16),
        wx_bd=jnp.stack([_block_diag(p["lru_wx"][l, d]) for d in range(2)]).astype(BF16),
        ba=p["lru_ba"][l].reshape(2, 1, LRU_WIDTH),
        bx=p["lru_bx"][l].reshape(2, 1, LRU_WIDTH),
        lam=p["lru_lambda"][l].reshape(2, 1, LRU_WIDTH),
        dw_w=p["dw_w"][l],
        dw_b=row(p["dw_b"][l]),
        cln_g=row(p["cln_g"][l]),
        cln_b=row(p["cln_b"][l]),
        lam_vec=p["lam_vec"][l],
        subln_g=row(p["subln_g"][l]),
        rw_t=rw.T,
        rb=rb.reshape(ROUTER_ROWS, 1),
    )


def _trunk(x, mods, layers, moe, final_g):
    bsz, s, _ = x.shape
    t = bsz * s
    tables = _rope_tables(s)
    x = x.reshape(t, D_MODEL)
    for l, w in enumerate(layers):
        mod = mods[l]
        ab, qk, vt = _in_proj(x, mod, w["norm1_g"], w["w_in"], tables, s)
        ab3 = ab.reshape(bsz, s, AB_COLS)
        ya = _lru(ab3, w["conv_w"], w["conv_b"], w["wa_bd"], w["ba"], w["wx_bd"], w["bx"], w["lam"])
        yb = _convmod(ab3, w["dw_w"], w["dw_b"], w["cln_g"], w["cln_b"])
        lam_init = 0.8 - 0.6 * math.exp(-0.3 * l)
        yc = _attn(qk.reshape(bsz, s, QK_COLS), vt, w["lam_vec"], w["subln_g"], lam_init)
        x, hp, meta, cnt = _out_proj(x, ya.reshape(t, LRU_WIDTH), yb.reshape(t, CONV_WIDTH),
                                     yc.reshape(t, ATT_WIDTH), w["w_out"], mod, w["norm2_g"], w["rw_t"], w["rb"], s)
        dest, eids, n_used, n_tiles = _route_tables(meta, cnt, t)
        xs = _dispatch(dest, hp, n_tiles * TOKEN_TILE)
        ys = _experts(eids, n_used, xs, moe["moe_w1"], moe["moe_w3"], moe["moe_w2"], l)
        x = _combine(dest, x, ys, mod, final_g, s, final=(l == DEPTH - 1))
    return x.reshape(bsz, s, D_MODEL)


def kernel(x_prompt, x_sample, c_prompt, c_sample, norm1_g, norm2_g, final_g, ada_w, ada_b, w_in, w_out, conv_w, conv_b, lru_wa, lru_ba, lru_wx, lru_bx, lru_lambda, dw_w, dw_b, cln_g, cln_b, lam_vec, subln_g, router_w1, router_b1, router_w2, router_b2, moe_w1, moe_w3, moe_w2):
    p = dict(norm1_g=norm1_g, norm2_g=norm2_g, w_in=w_in, w_out=w_out, conv_w=conv_w, conv_b=conv_b,
             lru_wa=lru_wa, lru_ba=lru_ba, lru_wx=lru_wx, lru_bx=lru_bx, lru_lambda=lru_lambda,
             dw_w=dw_w, dw_b=dw_b, cln_g=cln_g, cln_b=cln_b, lam_vec=lam_vec, subln_g=subln_g,
             router_w1=router_w1, router_b1=router_b1, router_w2=router_w2, router_b2=router_b2,
             moe_w1=moe_w1, moe_w3=moe_w3, moe_w2=moe_w2)
    layers = [_prep_layer(l, p) for l in range(DEPTH)]
    nb = c_prompt.shape[0]
    mods = _ada_mod(jnp.concatenate([c_prompt, c_sample], axis=0), ada_w, ada_b)
    fg = final_g.reshape(1, D_MODEL)
    moe = dict(moe_w1=moe_w1, moe_w3=moe_w3, moe_w2=moe_w2)
    y_prompt = _trunk(x_prompt, mods[:, :nb], layers, moe, fg)
    y_sample = _trunk(x_sample, mods[:, nb:], layers, moe, fg)
    return (y_prompt, y_sample)
```

```python
import functools
import math

import jax
import jax.numpy as jnp
from jax import lax
from jax.experimental import pallas as pl
from jax.experimental.pallas import tpu as pltpu

F32 = jnp.float32
BF16 = jnp.bfloat16

D_MODEL = 1024
DEPTH = 2
LRU_WIDTH = 256
LRU_HEADS = 4
LRU_CONV = 4
LRU_C = 8.0
CONV_WIDTH = 256
CONV_KERNEL = 31
ATT_WIDTH = 512
ATT_HEADS = 4
ATT_VDIM = 128
ATT_QKDIM = 64
ROPE_DIM = 16
ROPE_THETA = 500000.0
AB_COLS = 2 * LRU_WIDTH + 2 * CONV_WIDTH
QK_COLS = 2 * ATT_WIDTH
IN_COLS = AB_COLS + QK_COLS + ATT_WIDTH
N_GROUPS = 4
EXPERTS_PER_GROUP = 4
N_EXPERTS = 16
D_EXPERT = 512
EPS = 1e-6

LANES = 128
SUBLANES = 8
ROUTER_ROWS = 24
PAIRS_PER_GROUP = 6
N_BUCKETS = N_GROUPS * PAIRS_PER_GROUP
BUCKET_ROWS = 32
PAYLOAD_COLS = D_MODEL + LANES
VMEM_LIMIT = 56 * 1024 * 1024

TOKEN_TILE = 512
SEQ_CHUNK = 512
SCORE_TILE_BYTES = 16 * 1024 * 1024
MAX_Q_TILE = 512
ONES_ROWS = 16
KEY_CHUNK = 256
LRU_HALO = 8
CONV_HALO = 16


def _cparams(*sem):
    return pltpu.CompilerParams(dimension_semantics=sem, vmem_limit_bytes=VMEM_LIMIT)


def _rms(x, g):
    return x * lax.rsqrt(jnp.mean(x * x, axis=-1, keepdims=True) + EPS) * g


def _ada_kernel(c_ref, w_ref, b_ref, o_ref):
    c = c_ref[...]
    cs = (c * jax.nn.sigmoid(c)).astype(BF16)
    o_ref[...] = jnp.dot(cs, w_ref[...].astype(BF16), preferred_element_type=F32) + b_ref[...]


def _ada_mod(c, ada_w, ada_b):
    nb = c.shape[0]
    tn = 1536
    out = pl.pallas_call(
        _ada_kernel,
        out_shape=jax.ShapeDtypeStruct((DEPTH, nb, 6 * D_MODEL), F32),
        grid=(DEPTH, 6 * D_MODEL // tn),
        in_specs=[
            pl.BlockSpec((nb, D_MODEL), lambda l, j: (0, 0)),
            pl.BlockSpec((None, D_MODEL, tn), lambda l, j: (l, 0, j)),
            pl.BlockSpec((None, 1, tn), lambda l, j: (l, 0, j)),
        ],
        out_specs=pl.BlockSpec((None, nb, tn), lambda l, j: (l, 0, j)),
        compiler_params=_cparams("arbitrary", "arbitrary"),
        name="ada_mod",
    )(c, ada_w, ada_b.reshape(DEPTH, 1, 6 * D_MODEL))
    return out.reshape(DEPTH, nb, 6, D_MODEL)


def _in_proj_kernel(x_ref, mod_ref, g_ref, w_ref, cos_ref, sa_ref, sb_ref, ab_ref, qk_ref, vt_ref):
    x = x_ref[...]
    h = _rms(x, g_ref[...]) * (1.0 + mod_ref[1:2, :]) + mod_ref[0:1, :]
    hb = h.astype(BF16)
    ab_ref[...] = jnp.dot(hb, w_ref[:, 0:AB_COLS], preferred_element_type=F32)
    qk = jnp.dot(hb, w_ref[:, AB_COLS:AB_COLS + QK_COLS], preferred_element_type=F32)
    cos, sa, sb = cos_ref[...], sa_ref[...], sb_ref[...]
    for c in range(QK_COLS // LANES):
        blk = qk[:, c * LANES:(c + 1) * LANES]
        rot = (blk * cos + pltpu.roll(blk, ROPE_DIM // 2, 1) * sa
               + pltpu.roll(blk, LANES - ROPE_DIM // 2, 1) * sb)
        if c < ATT_WIDTH // LANES:
            rot = rot * (ATT_QKDIM ** -0.5)
        qk_ref[:, c * LANES:(c + 1) * LANES] = rot.astype(BF16)
    v = jnp.dot(hb, w_ref[:, AB_COLS + QK_COLS:], preferred_element_type=F32)
    vt_ref[...] = v.T.astype(BF16)


def _rope_tables(s):
    half = ROPE_DIM // 2
    inv = ROPE_THETA ** (-(jnp.arange(half, dtype=F32) * 2.0) / ROPE_DIM)
    ang = jnp.arange(s, dtype=jnp.int32).astype(F32)[:, None] * inv[None, :]
    cos, sin = jnp.cos(ang), jnp.sin(ang)
    rest = ATT_QKDIM - ROPE_DIM
    one = jnp.ones((s, rest), F32)
    zero = jnp.zeros((s, rest), F32)
    zh = jnp.zeros((s, half), F32)
    reps = LANES // ATT_QKDIM
    cos_t = jnp.tile(jnp.concatenate([cos, cos, one], axis=1), (1, reps))
    sa_t = jnp.tile(jnp.concatenate([zh, sin, zero], axis=1), (1, reps))
    sb_t = jnp.tile(jnp.concatenate([-sin, zh, zero], axis=1), (1, reps))
    return cos_t, sa_t, sb_t


def _in_proj(x, mod, g, w_in_bf, tables, s):
    t = x.shape[0]
    tm = TOKEN_TILE
    per_seq = s // tm
    row = lambda i: (i, 0)
    pos = lambda i: (i % per_seq, 0)
    return pl.pallas_call(
        _in_proj_kernel,
        out_shape=(
            jax.ShapeDtypeStruct((t, AB_COLS), F32),
            jax.ShapeDtypeStruct((t, QK_COLS), BF16),
            jax.ShapeDtypeStruct((t // s, ATT_WIDTH, s), BF16),
        ),
        grid=(t // tm,),
        in_specs=[
            pl.BlockSpec((tm, D_MODEL), row),
            pl.BlockSpec((None, 6, D_MODEL), lambda i: (i // per_seq, 0, 0)),
            pl.BlockSpec((1, D_MODEL), lambda i: (0, 0)),
            pl.BlockSpec((D_MODEL, IN_COLS), lambda i: (0, 0)),
            pl.BlockSpec((tm, LANES), pos),
            pl.BlockSpec((tm, LANES), pos),
            pl.BlockSpec((tm, LANES), pos),
        ],
        out_specs=(
            pl.BlockSpec((tm, AB_COLS), row),
            pl.BlockSpec((tm, QK_COLS), row),
            pl.BlockSpec((None, ATT_WIDTH, tm), lambda i: (i // per_seq, 0, i % per_seq)),
        ),
        compiler_params=_cparams("arbitrary"),
        name="in_proj",
    )(x, mod, g, w_in_bf, *tables)


def _neg_expm1(y):
    series = -y * (1.0 + y * (1.0 / 2.0) * (1.0 + y * (1.0 / 3.0) * (1.0 + y * (1.0 / 4.0) * (1.0 + y * (1.0 / 5.0)))))
    return jnp.where(y > -1.0 / 64.0, series, 1.0 - jnp.exp(y))


def _tile_scan(a, b, reverse):
    n = a.shape[0]
    a3 = a.reshape(n // SUBLANES, SUBLANES, a.shape[1])
    b3 = b.reshape(n // SUBLANES, SUBLANES, b.shape[1])
    row = lax.broadcasted_iota(jnp.int32, a3.shape, 1)
    for d in (1, 2, 4):
        shift = SUBLANES - d if reverse else d
        a_s = pltpu.roll(a3, shift, 1)
        b_s = pltpu.roll(b3, shift, 1)
        m = (row < SUBLANES - d) if reverse else (row >= d)
        b3 = jnp.where(m, b3 + a3 * b_s, b3)
        a3 = jnp.where(m, a3 * a_s, a3)
    return a3.reshape(a.shape), b3.reshape(b.shape)


def _lru_kernel(cur_ref, prev_ref, next_ref, ga_ref, cw_ref, cb_ref, wa_ref, ba_ref, wx_ref, bx_ref,
                lam_ref, o_ref, hf_ref, ext_ref, a_ref, b_ref, h_ref, carry_ref, *, n_chunks):
    tc = SEQ_CHUNK
    p = pl.program_id(1)
    j = pl.program_id(2)
    jx = j + p * (n_chunks - 1 - 2 * j)

    @pl.when(j == 0)
    def _():
        carry_ref[...] = jnp.zeros_like(carry_ref)

    zero_halo = jnp.zeros((LRU_HALO, LRU_WIDTH), F32)
    ext_ref[0:LRU_HALO, :] = jnp.where(jx == 0, zero_halo, prev_ref[...])
    ext_ref[LRU_HALO:LRU_HALO + tc, :] = cur_ref[...]
    ext_ref[LRU_HALO + tc:, :] = jnp.where(jx == n_chunks - 1, zero_halo, next_ref[...])
    xc = cb_ref[...]
    for k in range(LRU_CONV):
        off = LRU_HALO - 2 + k
        xc = xc + cw_ref[k:k + 1, :] * ext_ref[off:off + tc, :]
    xb = xc.astype(BF16)
    r = jax.nn.sigmoid(jnp.dot(xb, wa_ref[...], preferred_element_type=F32) + ba_ref[...])
    gate_i = jax.nn.sigmoid(jnp.dot(xb, wx_ref[...], preferred_element_type=F32) + bx_ref[...])
    log_a = (-LRU_C) * r * jax.nn.softplus(-lam_ref[...])
    a = jnp.exp(log_a)
    b = jnp.sqrt(_neg_expm1(2.0 * log_a)) * (gate_i * xc)

    n_tiles = tc // SUBLANES

    def run(reverse):
        a_s, b_s = _tile_scan(a, b, reverse)
        a_ref[...] = a_s
        b_ref[...] = b_s

        def body(i, hc):
            ti = (n_tiles - 1 - i) if reverse else i
            off = pl.multiple_of(ti * SUBLANES, SUBLANES)
            h = b_ref[pl.ds(off, SUBLANES), :] + a_ref[pl.ds(off, SUBLANES), :] * hc
            h_ref[pl.ds(off, SUBLANES), :] = h
            edge = h[0:1, :] if reverse else h[SUBLANES - 1:SUBLANES, :]
            return jnp.broadcast_to(edge, (SUBLANES, LRU_WIDTH))

        carry_ref[...] = lax.fori_loop(0, n_tiles, body, carry_ref[...], unroll=8)

    row0 = pl.multiple_of(jx * tc, tc)

    @pl.when(p == 0)
    def _():
        run(False)
        hf_ref[pl.ds(row0, tc), :] = h_ref[...]

    @pl.when(p == 1)
    def _():
        run(True)
        hsum = hf_ref[pl.ds(row0, tc), :] + h_ref[...]
        o_ref[...] = (jax.nn.gelu(ga_ref[...]) * hsum).astype(BF16)


def _lru(ab3, conv_w, conv_b, wa_bd, ba, wx_bd, bx, lam):
    bsz, s, _ = ab3.shape
    tc = SEQ_CHUNK
    n = s // tc
    hb = tc // LRU_HALO
    jx = lambda p, j: j + p * (n - 1 - 2 * j)
    jg = lambda p, j: n - 1 - p * j
    w2 = lambda b, p, j: (0, 0)
    wdir = lambda b, p, j: (p, 0, 0)
    return pl.pallas_call(
        functools.partial(_lru_kernel, n_chunks=n),
        out_shape=jax.ShapeDtypeStruct((bsz, s, LRU_WIDTH), BF16),
        grid=(bsz, 2, n),
        in_specs=[
            pl.BlockSpec((None, tc, LRU_WIDTH), lambda b, p, j: (b, jx(p, j), 0)),
            pl.BlockSpec((None, LRU_HALO, LRU_WIDTH),
                         lambda b, p, j: (b, jnp.maximum(jx(p, j) * hb - 1, 0), 0)),
            pl.BlockSpec((None, LRU_HALO, LRU_WIDTH),
                         lambda b, p, j: (b, jnp.minimum((jx(p, j) + 1) * hb, s // LRU_HALO - 1), 0)),
            pl.BlockSpec((None, tc, LRU_WIDTH), lambda b, p, j: (b, jg(p, j), 1)),
            pl.BlockSpec((LRU_CONV, LRU_WIDTH), w2),
            pl.BlockSpec((1, LRU_WIDTH), w2),
            pl.BlockSpec((None, LRU_WIDTH, LRU_WIDTH), wdir),
            pl.BlockSpec((None, 1, LRU_WIDTH), wdir),
            pl.BlockSpec((None, LRU_WIDTH, LRU_WIDTH), wdir),
            pl.BlockSpec((None, 1, LRU_WIDTH), wdir),
            pl.BlockSpec((None, 1, LRU_WIDTH), wdir),
        ],
        out_specs=pl.BlockSpec((None, tc, LRU_WIDTH), lambda b, p, j: (b, jg(p, j), 0)),
        scratch_shapes=[
            pltpu.VMEM((s, LRU_WIDTH), F32),
            pltpu.VMEM((tc + 2 * LRU_HALO, LRU_WIDTH), F32),
            pltpu.VMEM((tc, LRU_WIDTH), F32),
            pltpu.VMEM((tc, LRU_WIDTH), F32),
            pltpu.VMEM((tc, LRU_WIDTH), F32),
            pltpu.VMEM((SUBLANES, LRU_WIDTH), F32),
        ],
        compiler_params=_cparams("arbitrary", "arbitrary", "arbitrary"),
        name="lru",
    )(ab3, ab3, ab3, ab3, conv_w, conv_b, wa_bd, ba, wx_bd, bx, lam)


def _convmod_kernel(u_ref, g_ref, up_ref, gp_ref, un_ref, gn_ref, w_ref, b_ref, lg_ref, lb_ref,
                    o_ref, ext_ref, sh_ref, *, n_chunks):
    tc = SEQ_CHUNK
    j = pl.program_id(1)
    zero_halo = jnp.zeros((CONV_HALO, CONV_WIDTH), F32)
    glu = lambda u, g: u * jax.nn.sigmoid(g)
    ext_ref[0:CONV_HALO, :] = jnp.where(j == 0, zero_halo, glu(up_ref[...], gp_ref[...]))
    ext_ref[CONV_HALO:CONV_HALO + tc, :] = glu(u_ref[...], g_ref[...])
    ext_ref[CONV_HALO + tc:, :] = jnp.where(j == n_chunks - 1, zero_halo, glu(un_ref[...], gn_ref[...]))
    span = sh_ref.shape[1]
    for ph in range(1, SUBLANES):
        sh_ref[ph] = ext_ref[ph:ph + span, :]
    acc = b_ref[...]
    for k in range(CONV_KERNEL):
        off = CONV_HALO - CONV_KERNEL // 2 + k
        ph, base = off % SUBLANES, off - off % SUBLANES
        src = ext_ref[base:base + tc, :] if ph == 0 else sh_ref[ph, base:base + tc, :]
        acc = acc + w_ref[k:k + 1, :] * src
    mu = jnp.mean(acc, axis=-1, keepdims=True)
    cen = acc - mu
    var = jnp.mean(cen * cen, axis=-1, keepdims=True)
    z = cen * lax.rsqrt(var + EPS) * lg_ref[...] + lb_ref[...]
    o_ref[...] = (z * jax.nn.sigmoid(z)).astype(BF16)


def _convmod(ab3, dw_w, dw_b, cln_g, cln_b):
    bsz, s, _ = ab3.shape
    tc = SEQ_CHUNK
    n = s // tc
    hb = tc // CONV_HALO
    w2 = lambda b, j: (0, 0)
    ucol = 2 * LRU_WIDTH // CONV_WIDTH
    gcol = ucol + 1
    cur = lambda col: pl.BlockSpec((None, tc, CONV_WIDTH), lambda b, j: (b, j, col))
    prev = lambda col: pl.BlockSpec((None, CONV_HALO, CONV_WIDTH),
                                    lambda b, j: (b, jnp.maximum(j * hb - 1, 0), col))
    nxt = lambda col: pl.BlockSpec((None, CONV_HALO, CONV_WIDTH),
                                   lambda b, j: (b, jnp.minimum((j + 1) * hb, s // CONV_HALO - 1), col))
    return pl.pallas_call(
        functools.partial(_convmod_kernel, n_chunks=n),
        out_shape=jax.ShapeDtypeStruct((bsz, s, CONV_WIDTH), BF16),
        grid=(bsz, n),
        in_specs=[
            cur(ucol), cur(gcol), prev(ucol), prev(gcol), nxt(ucol), nxt(gcol),
            pl.BlockSpec((CONV_KERNEL, CONV_WIDTH), w2),
            pl.BlockSpec((1, CONV_WIDTH), w2),
            pl.BlockSpec((1, CONV_WIDTH), w2),
            pl.BlockSpec((1, CONV_WIDTH), w2),
        ],
        out_specs=pl.BlockSpec((None, tc, CONV_WIDTH), lambda b, j: (b, j, 0)),
        scratch_shapes=[pltpu.VMEM((tc + 2 * CONV_HALO, CONV_WIDTH), F32),
                        pltpu.VMEM((SUBLANES, tc + 2 * CONV_HALO - SUBLANES, CONV_WIDTH), F32)],
        compiler_params=_cparams("arbitrary", "arbitrary"),
        name="convmod",
    )(ab3, ab3, ab3, ab3, ab3, ab3, dw_w, dw_b, cln_g, cln_b)


def _attn_kernel(q_ref, k_ref, vt_ref, lv_ref, sg_ref, o_ref, st0_ref, st1_ref, m0_ref, m1_ref, *, lam_init):
    tq = q_ref.shape[0]
    g = pl.program_id(0)

    @pl.when(g == 0)
    def _():
        st1_ref[...] = jnp.zeros_like(st1_ref)
        m1_ref[...] = jnp.zeros_like(m1_ref)

    def step(st_new_ref, m_new_ref, st_old_ref, m_old_ref):
        q = q_ref[...]
        lane = lax.broadcasted_iota(jnp.int32, q.shape, 1)
        zero = jnp.zeros_like(q)
        qq = jnp.concatenate([jnp.where(lane < ATT_QKDIM, q, zero), jnp.where(lane >= ATT_QKDIM, q, zero)], axis=0)
        kc = KEY_CHUNK
        n_chunks = k_ref.shape[0] // kc
        ones = jnp.ones((ONES_ROWS, kc), BF16)
        m_old = m_old_ref[...]
        chunk = lambda c: slice(c * kc, (c + 1) * kc)

        def scores(c, m_new):
            st_c = lax.dot_general(k_ref[chunk(c), :], qq, (((1,), (1,)), ((), ())), preferred_element_type=F32)
            st_new_ref[chunk(c), :] = st_c
            m_c = jnp.max(st_c, axis=0, keepdims=True)
            return m_c if m_new is None else jnp.maximum(m_new, m_c)

        m_new = scores(0, None)
        ovt = None
        for c in range(n_chunks):
            e_c = jnp.exp(st_old_ref[chunk(c), :] - m_old).astype(BF16)
            if c + 1 < n_chunks:
                m_new = scores(c + 1, m_new)
            vt_c = jnp.concatenate([vt_ref[:, chunk(c)], ones], axis=0)
            o_c = jnp.dot(vt_c, e_c, preferred_element_type=F32)
            ovt = o_c if ovt is None else ovt + o_c
        m_new_ref[...] = m_new
        ovt = ovt[0:ATT_VDIM, :] / ovt[ATT_VDIM:ATT_VDIM + 1, :]
        lv = lv_ref[...]
        lam = (jnp.exp(jnp.sum(lv[0:1, :] * lv[1:2, :], axis=-1, keepdims=True))
               - jnp.exp(jnp.sum(lv[2:3, :] * lv[3:4, :], axis=-1, keepdims=True)) + lam_init)
        ot = ovt[:, 0:tq] - lam * ovt[:, tq:]
        ot = ot * lax.rsqrt(jnp.mean(ot * ot, axis=0, keepdims=True) + EPS) * sg_ref[...] * (1.0 - lam_init)
        o_ref[...] = ot.T.astype(BF16)

    @pl.when(g % 2 == 0)
    def _():
        step(st0_ref, m0_ref, st1_ref, m1_ref)

    @pl.when(g % 2 == 1)
    def _():
        step(st1_ref, m1_ref, st0_ref, m0_ref)


def _attn(qk3, vt3, lam_vec, subln_g, lam_init):
    bsz, s, _ = qk3.shape
    tq = min(MAX_Q_TILE, SCORE_TILE_BYTES // (2 * s * 4))
    n_q = s // tq
    n_tiles = bsz * ATT_HEADS * n_q
    kcol = ATT_WIDTH // LANES
    w2 = lambda g: (0, 0)
    cur = lambda g: jnp.minimum(g, n_tiles - 1)
    prv = lambda g: jnp.maximum(g - 1, 0)
    bat = lambda t: t // (ATT_HEADS * n_q)
    head = lambda t: (t // n_q) % ATT_HEADS
    qi = lambda t: t % n_q
    return pl.pallas_call(
        functools.partial(_attn_kernel, lam_init=lam_init),
        out_shape=jax.ShapeDtypeStruct((bsz, s, ATT_WIDTH), BF16),
        grid=(n_tiles + 1,),
        in_specs=[
            pl.BlockSpec((None, tq, LANES), lambda g: (bat(cur(g)), qi(cur(g)), head(cur(g)))),
            pl.BlockSpec((None, s, LANES), lambda g: (bat(cur(g)), 0, kcol + head(cur(g)))),
            pl.BlockSpec((None, ATT_VDIM, s), lambda g: (bat(prv(g)), head(prv(g)), 0)),
            pl.BlockSpec((4, ATT_QKDIM), w2),
            pl.BlockSpec((ATT_VDIM, 1), w2),
        ],
        out_specs=pl.BlockSpec((None, tq, ATT_VDIM), lambda g: (bat(prv(g)), qi(prv(g)), head(prv(g)))),
        scratch_shapes=[pltpu.VMEM((s, 2 * tq), F32), pltpu.VMEM((s, 2 * tq), F32),
                        pltpu.VMEM((1, 2 * tq), F32), pltpu.VMEM((1, 2 * tq), F32)],
        compiler_params=_cparams("arbitrary"),
        name="attn",
    )(qk3, qk3, vt3, lam_vec, subln_g.reshape(ATT_VDIM, 1))


def _first_argmax(vals):
    best = vals[0]
    idx = jnp.zeros(best.shape, jnp.int32)
    for i in range(1, len(vals)):
        better = vals[i] > best
        idx = jnp.where(better, i, idx)
        best = jnp.where(better, vals[i], best)
    return idx, best


def _out_proj_kernel(x_ref, ya_ref, yb_ref, yc_ref, w_ref, mod_ref, g_ref, rw_ref, rb_ref,
                     xo_ref, hp_ref, meta_ref, cnt_ref):
    tm = TOKEN_TILE
    y = jnp.dot(ya_ref[...], w_ref[0:LRU_WIDTH, :], preferred_element_type=F32)
    y = y + jnp.dot(yb_ref[...], w_ref[LRU_WIDTH:LRU_WIDTH + CONV_WIDTH, :], preferred_element_type=F32)
    y = y + jnp.dot(yc_ref[...], w_ref[LRU_WIDTH + CONV_WIDTH:, :], preferred_element_type=F32)
    x = x_ref[...] + mod_ref[2:3, :] * y
    xo_ref[...] = x
    h = _rms(x, g_ref[...]) * (1.0 + mod_ref[4:5, :]) + mod_ref[3:4, :]
    hp_ref[:, 0:D_MODEL] = h

    logits = lax.dot_general(rw_ref[...], h, (((1,), (1,)), ((), ())), precision=lax.Precision.HIGHEST,
                             preferred_element_type=F32) + rb_ref[...]
    gl = [logits[g:g + 1, :] for g in range(N_GROUPS)]
    g_idx, g_max = _first_argmax(gl)
    denom = gl[0] * 0.0
    for g in range(N_GROUPS):
        denom = denom + jnp.exp(gl[g] - g_max)
    g_w = 1.0 / denom
    fl = []
    for e in range(EXPERTS_PER_GROUP):
        sel = logits[N_GROUPS + e:N_GROUPS + e + 1, :]
        for g in range(1, N_GROUPS):
            r0 = N_GROUPS + g * EXPERTS_PER_GROUP + e
            sel = jnp.where(g_idx == g, logits[r0:r0 + 1, :], sel)
        fl.append(sel)
    i1, v1 = _first_argmax(fl)
    neg = jnp.full(v1.shape, -jnp.inf, F32)
    i2, v2 = _first_argmax([jnp.where(i1 == e, neg, fl[e]) for e in range(EXPERTS_PER_GROUP)])
    t = jnp.exp(v2 - v1)
    p1 = g_w / (1.0 + t)
    p2 = g_w * t / (1.0 + t)
    first_lo = i1 < i2
    lo = jnp.where(first_lo, i1, i2)
    hi = jnp.where(first_lo, i2, i1)
    pair = jnp.where(lo == 0, 0, jnp.where(lo == 1, 3, 5)) + hi - lo - 1
    bucket = g_idx * PAIRS_PER_GROUP + pair
    p_lo = jnp.where(first_lo, p1, p2)
    p_hi = jnp.where(first_lo, p2, p1)
    wrows = jnp.concatenate([p_lo, p_hi, jnp.zeros((LANES - 2, tm), F32)], axis=0)
    hp_ref[:, D_MODEL:] = wrows.T

    @pl.when(pl.program_id(0) == 0)
    def _():
        cnt_ref[...] = jnp.zeros_like(cnt_ref)

    onehot = lax.broadcasted_iota(jnp.int32, (BUCKET_ROWS, tm), 0) == bucket
    before = lax.broadcasted_iota(jnp.int32, (tm, tm), 0) < lax.broadcasted_iota(jnp.int32, (tm, tm), 1)
    prefix = jnp.dot(jnp.where(onehot, 1.0, 0.0).astype(BF16), jnp.where(before, 1.0, 0.0).astype(BF16),
                     preferred_element_type=F32)
    base = cnt_ref[:, 0:1]
    rank = jnp.sum(jnp.where(onehot, prefix + base, 0.0), axis=0, keepdims=True)
    cnt_ref[...] = cnt_ref[...] + jnp.sum(jnp.where(onehot, 1.0, 0.0), axis=1, keepdims=True)
    meta_ref[...] = jnp.concatenate([bucket.astype(F32), rank, jnp.zeros((SUBLANES - 2, tm), F32)], axis=0)


def _out_proj(x, ya, yb, yc, w_out_bf, mod, g2, rw_t, rb, s):
    t = x.shape[0]
    tm = TOKEN_TILE
    per_seq = s // tm
    row = lambda i: (i, 0)
    w2 = lambda i: (0, 0)
    return pl.pallas_call(
        _out_proj_kernel,
        out_shape=(
            jax.ShapeDtypeStruct((t, D_MODEL), F32),
            jax.ShapeDtypeStruct((t, PAYLOAD_COLS), F32),
            jax.ShapeDtypeStruct((SUBLANES, t), F32),
            jax.ShapeDtypeStruct((BUCKET_ROWS, LANES), F32),
        ),
        grid=(t // tm,),
        in_specs=[
            pl.BlockSpec((tm, D_MODEL), row),
            pl.BlockSpec((tm, LRU_WIDTH), row),
            pl.BlockSpec((tm, CONV_WIDTH), row),
            pl.BlockSpec((tm, ATT_WIDTH), row),
            pl.BlockSpec((D_MODEL, D_MODEL), w2),
            pl.BlockSpec((None, 6, D_MODEL), lambda i: (i // per_seq, 0, 0)),
            pl.BlockSpec((1, D_MODEL), w2),
            pl.BlockSpec((ROUTER_ROWS, D_MODEL), w2),
            pl.BlockSpec((ROUTER_ROWS, 1), w2),
        ],
        out_specs=(
            pl.BlockSpec((tm, D_MODEL), row),
            pl.BlockSpec((tm, PAYLOAD_COLS), row),
            pl.BlockSpec((SUBLANES, tm), lambda i: (0, i)),
            pl.BlockSpec((BUCKET_ROWS, LANES), w2),
        ),
        compiler_params=_cparams("arbitrary"),
        name="out_proj",
    )(x, ya, yb, yc, w_out_bf, mod, g2, rw_t, rb)


def _route_tables(meta, cnt, t):
    tm = TOKEN_TILE
    n_tiles = t // tm + N_BUCKETS
    counts = cnt[:N_BUCKETS, 0].astype(jnp.int32)
    tiles_per = (counts + tm - 1) // tm
    tile_end = jnp.cumsum(tiles_per)
    offs = (tile_end - tiles_per) * tm
    n_used = tile_end[-1:]
    bucket = meta[0].astype(jnp.int32)
    rank = meta[1].astype(jnp.int32)
    ids = jnp.arange(N_BUCKETS, dtype=jnp.int32)
    dest = rank + jnp.sum(jnp.where(bucket[:, None] == ids[None, :], offs[None, :], 0), axis=1)
    tile = jnp.minimum(jnp.arange(n_tiles, dtype=jnp.int32), n_used - 1)
    tb = jnp.sum((tile[:, None] >= tile_end[None, :]).astype(jnp.int32), axis=1)
    pair_lo = jnp.array([0, 0, 0, 1, 1, 2], jnp.int32)
    pair_hi = jnp.array([1, 2, 3, 2, 3, 3], jnp.int32)
    onehot_pair = (tb % PAIRS_PER_GROUP)[:, None] == jnp.arange(PAIRS_PER_GROUP, dtype=jnp.int32)[None, :]
    e_lo = (tb // PAIRS_PER_GROUP) * EXPERTS_PER_GROUP + jnp.sum(jnp.where(onehot_pair, pair_lo[None, :], 0), axis=1)
    e_hi = (tb // PAIRS_PER_GROUP) * EXPERTS_PER_GROUP + jnp.sum(jnp.where(onehot_pair, pair_hi[None, :], 0), axis=1)
    last_tile = jnp.where(counts > 0, tile_end - 1, -1)
    idle = n_used + jnp.arange(N_BUCKETS, dtype=jnp.int32)
    zero_tiles = jnp.concatenate([last_tile, jnp.where(idle < n_tiles, idle, -1)])
    return dest, zero_tiles, jnp.stack([e_lo, e_hi], axis=1).reshape(-1), n_used, n_tiles


def _row_copy(src_ref, src_row, dst_ref, dst_row, sem):
    return pltpu.make_async_copy(src_ref.at[pl.ds(src_row, 1), :], dst_ref.at[pl.ds(dst_row, 1), :], sem)


def _issue_rows(n_rows, start_row_copy):
    for r in range(n_rows):
        start_row_copy(r, r % 2)


def _dispatch_kernel(dest_ref, ztile_ref, hp_ref, xs_ref, zbuf_ref, sem, zsem):
    tm = TOKEN_TILE
    base = pl.program_id(0) * tm

    @pl.when(pl.program_id(0) == 0)
    def _():
        zbuf_ref[...] = jnp.zeros_like(zbuf_ref)

        def fill(k, carry):
            @pl.when(ztile_ref[k] >= 0)
            def _():
                row0 = pl.multiple_of(ztile_ref[k] * tm, tm)
                pltpu.make_async_copy(zbuf_ref, xs_ref.at[pl.ds(row0, tm), :], zsem).start()

            return carry

        def drain(k, carry):
            @pl.when(ztile_ref[k] >= 0)
            def _():
                pltpu.make_async_copy(zbuf_ref, xs_ref.at[pl.ds(0, tm), :], zsem).wait()

            return carry

        lax.fori_loop(0, 2 * N_BUCKETS, fill, 0)
        lax.fori_loop(0, 2 * N_BUCKETS, drain, 0)

    _issue_rows(tm, lambda r, pri: _row_copy(hp_ref, r, xs_ref, dest_ref[base + r], sem).start(priority=pri))
    pltpu.make_async_copy(hp_ref, xs_ref.at[pl.ds(0, tm), :], sem).wait()


def _dispatch(dest, zero_tiles, hp, n_rows):
    t = hp.shape[0]
    tm = TOKEN_TILE
    return pl.pallas_call(
        _dispatch_kernel,
        out_shape=jax.ShapeDtypeStruct((n_rows, PAYLOAD_COLS), F32),
        grid_spec=pltpu.PrefetchScalarGridSpec(
            num_scalar_prefetch=2,
            grid=(t // tm,),
            in_specs=[pl.BlockSpec((tm, PAYLOAD_COLS), lambda i, dest, zt: (i, 0))],
            out_specs=pl.BlockSpec(memory_space=pl.ANY),
            scratch_shapes=[pltpu.VMEM((tm, PAYLOAD_COLS), F32), pltpu.SemaphoreType.DMA, pltpu.SemaphoreType.DMA],
        ),
        compiler_params=_cparams("arbitrary"),
        name="dispatch",
    )(dest, zero_tiles, hp)


def _experts_kernel(eid_ref, nused_ref, x_ref, w1a_ref, w3a_ref, w2a_ref, w1b_ref, w3b_ref, w2b_ref, o_ref,
                    w1a_bf, w3a_bf, w2a_bf, w1b_bf, w3b_bf, w2b_bf):
    i = pl.program_id(0)
    busy = i < nused_ref[0]
    prev = jnp.maximum(i - 1, 0)

    def refresh(slot, pairs):
        @pl.when(jnp.logical_or(i == 0, eid_ref[2 * i + slot] != eid_ref[2 * prev + slot]))
        def _():
            for w_ref, w_bf in pairs:
                w_bf[...] = w_ref[...].astype(BF16)

    refresh(0, ((w1a_ref, w1a_bf), (w3a_ref, w3a_bf), (w2a_ref, w2a_bf)))
    refresh(1, ((w1b_ref, w1b_bf), (w3b_ref, w3b_bf), (w2b_ref, w2b_bf)))

    @pl.when(jnp.logical_not(busy))
    def _():
        o_ref[...] = jnp.zeros_like(o_ref)

    @pl.when(busy)
    def _():
        x = x_ref[:, 0:D_MODEL].astype(BF16)

        def expert(w1_bf, w3_bf, w2_bf):
            a = jnp.dot(x, w1_bf[...], preferred_element_type=F32)
            he = (a * jax.nn.sigmoid(a)) * jnp.dot(x, w3_bf[...], preferred_element_type=F32)
            return jnp.dot(he.astype(BF16), w2_bf[...], preferred_element_type=F32)

        o_ref[...] = (x_ref[:, D_MODEL:D_MODEL + 1] * expert(w1a_bf, w3a_bf, w2a_bf)
                      + x_ref[:, D_MODEL + 1:D_MODEL + 2] * expert(w1b_bf, w3b_bf, w2b_bf))


def _experts(eids, n_used, xs, w1, w3, w2, layer):
    n_rows = xs.shape[0]
    tm = TOKEN_TILE
    rows = lambda i, eid, nu: (jnp.maximum(jnp.minimum(i, nu[0] - 1), 0), 0)
    wspec = lambda shape, slot: pl.BlockSpec((None, None) + shape,
                                             lambda i, eid, nu: (layer, eid[2 * i + slot], 0, 0))
    up, down = (D_MODEL, D_EXPERT), (D_EXPERT, D_MODEL)
    return pl.pallas_call(
        _experts_kernel,
        out_shape=jax.ShapeDtypeStruct((n_rows, D_MODEL), F32),
        grid_spec=pltpu.PrefetchScalarGridSpec(
            num_scalar_prefetch=2,
            grid=(n_rows // tm,),
            in_specs=[
                pl.BlockSpec((tm, PAYLOAD_COLS), rows),
                wspec(up, 0), wspec(up, 0), wspec(down, 0),
                wspec(up, 1), wspec(up, 1), wspec(down, 1),
            ],
            out_specs=pl.BlockSpec((tm, D_MODEL), lambda i, eid, nu: (i, 0)),
            scratch_shapes=[pltpu.VMEM(up, BF16), pltpu.VMEM(up, BF16), pltpu.VMEM(down, BF16),
                            pltpu.VMEM(up, BF16), pltpu.VMEM(up, BF16), pltpu.VMEM(down, BF16)],
        ),
        compiler_params=_cparams("arbitrary"),
        name="experts",
    )(eids, n_used, xs, w1, w3, w2, w1, w3, w2)


def _combine_kernel(dest_ref, x_ref, ys_ref, mod_ref, g_ref, o_ref, ybuf_ref, sem, *, final):
    tm = TOKEN_TILE
    base = pl.program_id(0) * tm

    _issue_rows(tm, lambda r, pri: _row_copy(ys_ref, dest_ref[base + r], ybuf_ref, r, sem).start(priority=pri))
    pltpu.make_async_copy(ys_ref.at[pl.ds(0, tm), :], ybuf_ref, sem).wait()
    x = x_ref[...] + mod_ref[5:6, :] * ybuf_ref[...]
    o_ref[...] = _rms(x, g_ref[...]) if final else x


def _combine(dest, x, ys, mod, final_g, s, final):
    t = x.shape[0]
    tm = TOKEN_TILE
    per_seq = s // tm
    return pl.pallas_call(
        functools.partial(_combine_kernel, final=final),
        out_shape=jax.ShapeDtypeStruct((t, D_MODEL), F32),
        grid_spec=pltpu.PrefetchScalarGridSpec(
            num_scalar_prefetch=1,
            grid=(t // tm,),
            in_specs=[
                pl.BlockSpec((tm, D_MODEL), lambda i, dest: (i, 0)),
                pl.BlockSpec(memory_space=pl.ANY),
                pl.BlockSpec((None, 6, D_MODEL), lambda i, dest: (i // per_seq, 0, 0)),
                pl.BlockSpec((1, D_MODEL), lambda i, dest: (0, 0)),
            ],
            out_specs=pl.BlockSpec((tm, D_MODEL), lambda i, dest: (i, 0)),
            scratch_shapes=[pltpu.VMEM((tm, D_MODEL), F32), pltpu.SemaphoreType.DMA],
        ),
        compiler_params=_cparams("arbitrary"),
        name="combine",
    )(dest, x, ys, mod, final_g)


def _block_diag(w):
    h, dh, _ = w.shape
    eye = jnp.eye(h, dtype=w.dtype)
    return (eye[:, None, :, None] * w[:, :, None, :]).reshape(h * dh, h * dh)


def _prep_layer(l, p):
    row = lambda a: a.reshape(1, -1)
    rw = jnp.concatenate([p["router_w1"][l], p["router_w2"][l].reshape(D_MODEL, N_EXPERTS)], axis=1)
    rw = jnp.pad(rw, ((0, 0), (0, ROUTER_ROWS - rw.shape[1])))
    rb = jnp.concatenate([p["router_b1"][l], p["router_b2"][l].reshape(N_EXPERTS)])
    rb = jnp.pad(rb, (0, ROUTER_ROWS - rb.shape[0]))
    return dict(
        norm1_g=row(p["norm1_g"][l]),
        norm2_g=row(p["norm2_g"][l]),
        w_in=p["w_in"][l].astype(BF16),
        w_out=p["w_out"][l].astype(BF16),
        conv_w=p["conv_w"][l],
        conv_b=row(p["conv_b"][l]),
        wa_bd=jnp.stack([_block_diag(p["lru_wa"][l, d]) for d in range(2)]).astype(BF16),
        wx_bd=jnp.stack([_block_diag(p["lru_wx"][l, d]) for d in range(2)]).astype(BF16),
        ba=p["lru_ba"][l].reshape(2, 1, LRU_WIDTH),
        bx=p["lru_bx"][l].reshape(2, 1, LRU_WIDTH),
        lam=p["lru_lambda"][l].reshape(2, 1, LRU_WIDTH),
        dw_w=p["dw_w"][l],
        dw_b=row(p["dw_b"][l]),
        cln_g=row(p["cln_g"][l]),
        cln_b=row(p["cln_b"][l]),
        lam_vec=p["lam_vec"][l],
        subln_g=row(p["subln_g"][l]),
        rw_t=rw.T,
        rb=rb.reshape(ROUTER_ROWS, 1),
    )


def _trunk(x, mods, layers, moe, final_g):
    bsz, s, _ = x.shape
    t = bsz * s
    tables = _rope_tables(s)
    x = x.reshape(t, D_MODEL)
    for l, w in enumerate(layers):
        mod = mods[l]
        ab, qk, vt = _in_proj(x, mod, w["norm1_g"], w["w_in"], tables, s)
        ab3 = ab.reshape(bsz, s, AB_COLS)
        ya = _lru(ab3, w["conv_w"], w["conv_b"], w["wa_bd"], w["ba"], w["wx_bd"], w["bx"], w["lam"])
        yb = _convmod(ab3, w["dw_w"], w["dw_b"], w["cln_g"], w["cln_b"])
        lam_init = 0.8 - 0.6 * math.exp(-0.3 * l)
        yc = _attn(qk.reshape(bsz, s, QK_COLS), vt, w["lam_vec"], w["subln_g"], lam_init)
        x, hp, meta, cnt = _out_proj(x, ya.reshape(t, LRU_WIDTH), yb.reshape(t, CONV_WIDTH),
                                     yc.reshape(t, ATT_WIDTH), w["w_out"], mod, w["norm2_g"], w["rw_t"], w["rb"], s)
        dest, zero_tiles, eids, n_used, n_tiles = _route_tables(meta, cnt, t)
        xs = _dispatch(dest, zero_tiles, hp, n_tiles * TOKEN_TILE)
        ys = _experts(eids, n_used, xs, moe["moe_w1"], moe["moe_w3"], moe["moe_w2"], l)
        x = _combine(dest, x, ys, mod, final_g, s, final=(l == DEPTH - 1))
    return x.reshape(bsz, s, D_MODEL)


def kernel(x_prompt, x_sample, c_prompt, c_sample, norm1_g, norm2_g, final_g, ada_w, ada_b, w_in, w_out, conv_w, conv_b, lru_wa, lru_ba, lru_wx, lru_bx, lru_lambda, dw_w, dw_b, cln_g, cln_b, lam_vec, subln_g, router_w1, router_b1, router_w2, router_b2, moe_w1, moe_w3, moe_w2):
    p = dict(norm1_g=norm1_g, norm2_g=norm2_g, w_in=w_in, w_out=w_out, conv_w=conv_w, conv_b=conv_b,
             lru_wa=lru_wa, lru_ba=lru_ba, lru_wx=lru_wx, lru_bx=lru_bx, lru_lambda=lru_lambda,
             dw_w=dw_w, dw_b=dw_b, cln_g=cln_g, cln_b=cln_b, lam_vec=lam_vec, subln_g=subln_g,
             router_w1=router_w1, router_b1=router_b1, router_w2=router_w2, router_b2=router_b2,
             moe_w1=moe_w1, moe_w3=moe_w3, moe_w2=moe_w2)
    layers = [_prep_layer(l, p) for l in range(DEPTH)]
    nb = c_prompt.shape[0]
    mods = _ada_mod(jnp.concatenate([c_prompt, c_sample], axis=0), ada_w, ada_b)
    fg = final_g.reshape(1, D_MODEL)
    moe = dict(moe_w1=moe_w1, moe_w3=moe_w3, moe_w2=moe_w2)
    y_prompt = _trunk(x_prompt, mods[:, :nb], layers, moe, fg)
    y_sample = _trunk(x_sample, mods[:, nb:], layers, moe, fg)
    return (y_prompt, y_sample)
```

```python
import functools
import math

import jax
import jax.numpy as jnp
from jax import lax
from jax.experimental import pallas as pl
from jax.experimental.pallas import tpu as pltpu

F32 = jnp.float32
BF16 = jnp.bfloat16

D_MODEL = 1024
DEPTH = 2
LRU_WIDTH = 256
LRU_HEADS = 4
LRU_CONV = 4
LRU_C = 8.0
CONV_WIDTH = 256
CONV_KERNEL = 31
ATT_WIDTH = 512
ATT_HEADS = 4
ATT_VDIM = 128
ATT_QKDIM = 64
ROPE_DIM = 16
ROPE_THETA = 500000.0
AB_COLS = 2 * LRU_WIDTH + 2 * CONV_WIDTH
QK_COLS = 2 * ATT_WIDTH
IN_COLS = AB_COLS + QK_COLS + ATT_WIDTH
N_GROUPS = 4
EXPERTS_PER_GROUP = 4
N_EXPERTS = 16
D_EXPERT = 512
EPS = 1e-6
LOG2_E = 1.4426950408889634

LANES = 128
SUBLANES = 8
ROUTER_ROWS = 24
PAIRS_PER_GROUP = 6
N_BUCKETS = N_GROUPS * PAIRS_PER_GROUP
BUCKET_ROWS = 32
PAYLOAD_COLS = D_MODEL + LANES
VMEM_LIMIT = 56 * 1024 * 1024

TOKEN_TILE = 512
SEQ_CHUNK = 512
SCORE_TILE_BYTES = 16 * 1024 * 1024
MAX_Q_TILE = 512
ONES_ROWS = 16
KEY_CHUNK = 256
LRU_HALO = 8
CONV_HALO = 16


def _cparams(*sem):
    return pltpu.CompilerParams(dimension_semantics=sem, vmem_limit_bytes=VMEM_LIMIT)


def _rms(x, g):
    return x * lax.rsqrt(jnp.mean(x * x, axis=-1, keepdims=True) + EPS) * g


def _ada_kernel(c_ref, w_ref, b_ref, o_ref):
    c = c_ref[...]
    cs = (c * jax.nn.sigmoid(c)).astype(BF16)
    o_ref[...] = jnp.dot(cs, w_ref[...].astype(BF16), preferred_element_type=F32) + b_ref[...]


def _ada_mod(c, ada_w, ada_b):
    nb = c.shape[0]
    tn = 1536
    out = pl.pallas_call(
        _ada_kernel,
        out_shape=jax.ShapeDtypeStruct((DEPTH, nb, 6 * D_MODEL), F32),
        grid=(DEPTH, 6 * D_MODEL // tn),
        in_specs=[
            pl.BlockSpec((nb, D_MODEL), lambda l, j: (0, 0)),
            pl.BlockSpec((None, D_MODEL, tn), lambda l, j: (l, 0, j)),
            pl.BlockSpec((None, 1, tn), lambda l, j: (l, 0, j)),
        ],
        out_specs=pl.BlockSpec((None, nb, tn), lambda l, j: (l, 0, j)),
        compiler_params=_cparams("arbitrary", "arbitrary"),
        name="ada_mod",
    )(c, ada_w, ada_b.reshape(DEPTH, 1, 6 * D_MODEL))
    return out.reshape(DEPTH, nb, 6, D_MODEL)


def _in_proj_kernel(x_ref, mod_ref, g_ref, w_ref, cos_ref, sa_ref, sb_ref, ab_ref, qk_ref, vt_ref):
    x = x_ref[...]
    h = _rms(x, g_ref[...]) * (1.0 + mod_ref[1:2, :]) + mod_ref[0:1, :]
    hb = h.astype(BF16)
    ab_ref[...] = jnp.dot(hb, w_ref[:, 0:AB_COLS], preferred_element_type=F32)
    qk = jnp.dot(hb, w_ref[:, AB_COLS:AB_COLS + QK_COLS], preferred_element_type=F32)
    cos, sa, sb = cos_ref[...], sa_ref[...], sb_ref[...]
    for c in range(QK_COLS // LANES):
        blk = qk[:, c * LANES:(c + 1) * LANES]
        rot = (blk * cos + pltpu.roll(blk, ROPE_DIM // 2, 1) * sa
               + pltpu.roll(blk, LANES - ROPE_DIM // 2, 1) * sb)
        if c < ATT_WIDTH // LANES:
            rot = rot * (ATT_QKDIM ** -0.5 * LOG2_E)
        qk_ref[:, c * LANES:(c + 1) * LANES] = rot.astype(BF16)
    v = jnp.dot(hb, w_ref[:, AB_COLS + QK_COLS:], preferred_element_type=F32)
    vt_ref[...] = v.T.astype(BF16)


def _rope_tables(s):
    half = ROPE_DIM // 2
    inv = ROPE_THETA ** (-(jnp.arange(half, dtype=F32) * 2.0) / ROPE_DIM)
    ang = jnp.arange(s, dtype=jnp.int32).astype(F32)[:, None] * inv[None, :]
    cos, sin = jnp.cos(ang), jnp.sin(ang)
    rest = ATT_QKDIM - ROPE_DIM
    one = jnp.ones((s, rest), F32)
    zero = jnp.zeros((s, rest), F32)
    zh = jnp.zeros((s, half), F32)
    reps = LANES // ATT_QKDIM
    cos_t = jnp.tile(jnp.concatenate([cos, cos, one], axis=1), (1, reps))
    sa_t = jnp.tile(jnp.concatenate([zh, sin, zero], axis=1), (1, reps))
    sb_t = jnp.tile(jnp.concatenate([-sin, zh, zero], axis=1), (1, reps))
    return cos_t, sa_t, sb_t


def _in_proj(x, mod, g, w_in_bf, tables, s):
    t = x.shape[0]
    tm = TOKEN_TILE
    per_seq = s // tm
    row = lambda i: (i, 0)
    pos = lambda i: (i % per_seq, 0)
    return pl.pallas_call(
        _in_proj_kernel,
        out_shape=(
            jax.ShapeDtypeStruct((t, AB_COLS), F32),
            jax.ShapeDtypeStruct((t, QK_COLS), BF16),
            jax.ShapeDtypeStruct((t // s, ATT_WIDTH, s), BF16),
        ),
        grid=(t // tm,),
        in_specs=[
            pl.BlockSpec((tm, D_MODEL), row),
            pl.BlockSpec((None, 6, D_MODEL), lambda i: (i // per_seq, 0, 0)),
            pl.BlockSpec((1, D_MODEL), lambda i: (0, 0)),
            pl.BlockSpec((D_MODEL, IN_COLS), lambda i: (0, 0)),
            pl.BlockSpec((tm, LANES), pos),
            pl.BlockSpec((tm, LANES), pos),
            pl.BlockSpec((tm, LANES), pos),
        ],
        out_specs=(
            pl.BlockSpec((tm, AB_COLS), row),
            pl.BlockSpec((tm, QK_COLS), row),
            pl.BlockSpec((None, ATT_WIDTH, tm), lambda i: (i // per_seq, 0, i % per_seq)),
        ),
        compiler_params=_cparams("arbitrary"),
        name="in_proj",
    )(x, mod, g, w_in_bf, *tables)


def _neg_expm1(y, exp_half_y):
    series = -y * (1.0 + y * (1.0 / 2.0) * (1.0 + y * (1.0 / 3.0) * (1.0 + y * (1.0 / 4.0) * (1.0 + y * (1.0 / 5.0)))))
    return jnp.where(y > -1.0 / 64.0, series, 1.0 - exp_half_y * exp_half_y)


def _tile_scan(a, b, reverse):
    n = a.shape[0]
    a3 = a.reshape(n // SUBLANES, SUBLANES, a.shape[1])
    b3 = b.reshape(n // SUBLANES, SUBLANES, b.shape[1])
    row = lax.broadcasted_iota(jnp.int32, a3.shape, 1)
    for d in (1, 2, 4):
        shift = SUBLANES - d if reverse else d
        a_s = pltpu.roll(a3, shift, 1)
        b_s = pltpu.roll(b3, shift, 1)
        m = (row < SUBLANES - d) if reverse else (row >= d)
        b3 = jnp.where(m, b3 + a3 * b_s, b3)
        a3 = jnp.where(m, a3 * a_s, a3)
    return a3.reshape(a.shape), b3.reshape(b.shape)


def _lru_kernel(cur_ref, prev_ref, next_ref, ga_ref, cw_ref, cb_ref, wa_ref, ba_ref, wx_ref, bx_ref,
                lam_ref, o_ref, hf_ref, ext_ref, a_ref, b_ref, h_ref, carry_ref, *, n_chunks):
    tc = SEQ_CHUNK
    p = pl.program_id(1)
    j = pl.program_id(2)
    jx = j + p * (n_chunks - 1 - 2 * j)

    @pl.when(j == 0)
    def _():
        carry_ref[...] = jnp.zeros_like(carry_ref)

    zero_halo = jnp.zeros((LRU_HALO, LRU_WIDTH), F32)
    ext_ref[0:LRU_HALO, :] = jnp.where(jx == 0, zero_halo, prev_ref[...])
    ext_ref[LRU_HALO:LRU_HALO + tc, :] = cur_ref[...]
    ext_ref[LRU_HALO + tc:, :] = jnp.where(jx == n_chunks - 1, zero_halo, next_ref[...])
    xc = cb_ref[...]
    for k in range(LRU_CONV):
        off = LRU_HALO - 2 + k
        xc = xc + cw_ref[k:k + 1, :] * ext_ref[off:off + tc, :]
    xb = xc.astype(BF16)
    r = jax.nn.sigmoid(jnp.dot(xb, wa_ref[...], preferred_element_type=F32) + ba_ref[...])
    gate_i = jax.nn.sigmoid(jnp.dot(xb, wx_ref[...], preferred_element_type=F32) + bx_ref[...])
    log_a = (-LRU_C) * r * jax.nn.softplus(-lam_ref[...])
    a = jnp.exp(log_a)
    b = jnp.sqrt(_neg_expm1(2.0 * log_a, a)) * (gate_i * xc)

    n_tiles = tc // SUBLANES

    def run(reverse):
        a_s, b_s = _tile_scan(a, b, reverse)
        a_ref[...] = a_s
        b_ref[...] = b_s

        def body(i, hc):
            ti = (n_tiles - 1 - i) if reverse else i
            off = pl.multiple_of(ti * SUBLANES, SUBLANES)
            h = b_ref[pl.ds(off, SUBLANES), :] + a_ref[pl.ds(off, SUBLANES), :] * hc
            h_ref[pl.ds(off, SUBLANES), :] = h
            edge = h[0:1, :] if reverse else h[SUBLANES - 1:SUBLANES, :]
            return jnp.broadcast_to(edge, (SUBLANES, LRU_WIDTH))

        carry_ref[...] = lax.fori_loop(0, n_tiles, body, carry_ref[...], unroll=8)

    row0 = pl.multiple_of(jx * tc, tc)

    @pl.when(p == 0)
    def _():
        run(False)
        hf_ref[pl.ds(row0, tc), :] = h_ref[...]

    @pl.when(p == 1)
    def _():
        run(True)
        hsum = hf_ref[pl.ds(row0, tc), :] + h_ref[...]
        o_ref[...] = (jax.nn.gelu(ga_ref[...]) * hsum).astype(BF16)


def _lru(ab3, conv_w, conv_b, wa_bd, ba, wx_bd, bx, lam):
    bsz, s, _ = ab3.shape
    tc = SEQ_CHUNK
    n = s // tc
    hb = tc // LRU_HALO
    jx = lambda p, j: j + p * (n - 1 - 2 * j)
    jg = lambda p, j: n - 1 - p * j
    w2 = lambda b, p, j: (0, 0)
    wdir = lambda b, p, j: (p, 0, 0)
    return pl.pallas_call(
        functools.partial(_lru_kernel, n_chunks=n),
        out_shape=jax.ShapeDtypeStruct((bsz, s, LRU_WIDTH), BF16),
        grid=(bsz, 2, n),
        in_specs=[
            pl.BlockSpec((None, tc, LRU_WIDTH), lambda b, p, j: (b, jx(p, j), 0)),
            pl.BlockSpec((None, LRU_HALO, LRU_WIDTH),
                         lambda b, p, j: (b, jnp.maximum(jx(p, j) * hb - 1, 0), 0)),
            pl.BlockSpec((None, LRU_HALO, LRU_WIDTH),
                         lambda b, p, j: (b, jnp.minimum((jx(p, j) + 1) * hb, s // LRU_HALO - 1), 0)),
            pl.BlockSpec((None, tc, LRU_WIDTH), lambda b, p, j: (b, jg(p, j), 1)),
            pl.BlockSpec((LRU_CONV, LRU_WIDTH), w2),
            pl.BlockSpec((1, LRU_WIDTH), w2),
            pl.BlockSpec((None, LRU_WIDTH, LRU_WIDTH), wdir),
            pl.BlockSpec((None, 1, LRU_WIDTH), wdir),
            pl.BlockSpec((None, LRU_WIDTH, LRU_WIDTH), wdir),
            pl.BlockSpec((None, 1, LRU_WIDTH), wdir),
            pl.BlockSpec((None, 1, LRU_WIDTH), wdir),
        ],
        out_specs=pl.BlockSpec((None, tc, LRU_WIDTH), lambda b, p, j: (b, jg(p, j), 0)),
        scratch_shapes=[
            pltpu.VMEM((s, LRU_WIDTH), F32),
            pltpu.VMEM((tc + 2 * LRU_HALO, LRU_WIDTH), F32),
            pltpu.VMEM((tc, LRU_WIDTH), F32),
            pltpu.VMEM((tc, LRU_WIDTH), F32),
            pltpu.VMEM((tc, LRU_WIDTH), F32),
            pltpu.VMEM((SUBLANES, LRU_WIDTH), F32),
        ],
        compiler_params=_cparams("arbitrary", "arbitrary", "arbitrary"),
        name="lru",
    )(ab3, ab3, ab3, ab3, conv_w, conv_b, wa_bd, ba, wx_bd, bx, lam)


def _convmod_kernel(u_ref, g_ref, up_ref, gp_ref, un_ref, gn_ref, w_ref, b_ref, lg_ref, lb_ref,
                    o_ref, ext_ref, sh_ref, *, n_chunks):
    tc = SEQ_CHUNK
    j = pl.program_id(1)
    zero_halo = jnp.zeros((CONV_HALO, CONV_WIDTH), F32)
    glu = lambda u, g: u * jax.nn.sigmoid(g)
    ext_ref[0:CONV_HALO, :] = jnp.where(j == 0, zero_halo, glu(up_ref[...], gp_ref[...]))
    ext_ref[CONV_HALO:CONV_HALO + tc, :] = glu(u_ref[...], g_ref[...])
    ext_ref[CONV_HALO + tc:, :] = jnp.where(j == n_chunks - 1, zero_halo, glu(un_ref[...], gn_ref[...]))
    span = sh_ref.shape[1]
    for ph in range(1, SUBLANES):
        sh_ref[ph] = ext_ref[ph:ph + span, :]
    acc = b_ref[...]
    for k in range(CONV_KERNEL):
        off = CONV_HALO - CONV_KERNEL // 2 + k
        ph, base = off % SUBLANES, off - off % SUBLANES
        src = ext_ref[base:base + tc, :] if ph == 0 else sh_ref[ph, base:base + tc, :]
        acc = acc + w_ref[k:k + 1, :] * src
    mu = jnp.mean(acc, axis=-1, keepdims=True)
    cen = acc - mu
    var = jnp.mean(cen * cen, axis=-1, keepdims=True)
    z = cen * lax.rsqrt(var + EPS) * lg_ref[...] + lb_ref[...]
    o_ref[...] = (z * jax.nn.sigmoid(z)).astype(BF16)


def _convmod(ab3, dw_w, dw_b, cln_g, cln_b):
    bsz, s, _ = ab3.shape
    tc = SEQ_CHUNK
    n = s // tc
    hb = tc // CONV_HALO
    w2 = lambda b, j: (0, 0)
    ucol = 2 * LRU_WIDTH // CONV_WIDTH
    gcol = ucol + 1
    cur = lambda col: pl.BlockSpec((None, tc, CONV_WIDTH), lambda b, j: (b, j, col))
    prev = lambda col: pl.BlockSpec((None, CONV_HALO, CONV_WIDTH),
                                    lambda b, j: (b, jnp.maximum(j * hb - 1, 0), col))
    nxt = lambda col: pl.BlockSpec((None, CONV_HALO, CONV_WIDTH),
                                   lambda b, j: (b, jnp.minimum((j + 1) * hb, s // CONV_HALO - 1), col))
    return pl.pallas_call(
        functools.partial(_convmod_kernel, n_chunks=n),
        out_shape=jax.ShapeDtypeStruct((bsz, s, CONV_WIDTH), BF16),
        grid=(bsz, n),
        in_specs=[
            cur(ucol), cur(gcol), prev(ucol), prev(gcol), nxt(ucol), nxt(gcol),
            pl.BlockSpec((CONV_KERNEL, CONV_WIDTH), w2),
            pl.BlockSpec((1, CONV_WIDTH), w2),
            pl.BlockSpec((1, CONV_WIDTH), w2),
            pl.BlockSpec((1, CONV_WIDTH), w2),
        ],
        out_specs=pl.BlockSpec((None, tc, CONV_WIDTH), lambda b, j: (b, j, 0)),
        scratch_shapes=[pltpu.VMEM((tc + 2 * CONV_HALO, CONV_WIDTH), F32),
                        pltpu.VMEM((SUBLANES, tc + 2 * CONV_HALO - SUBLANES, CONV_WIDTH), F32)],
        compiler_params=_cparams("arbitrary", "arbitrary"),
        name="convmod",
    )(ab3, ab3, ab3, ab3, ab3, ab3, dw_w, dw_b, cln_g, cln_b)


def _attn_kernel(q_ref, k_ref, vt_ref, lv_ref, sg_ref, o_ref, st0_ref, st1_ref, m0_ref, m1_ref, *, lam_init):
    tq = q_ref.shape[0]
    g = pl.program_id(0)

    @pl.when(g == 0)
    def _():
        st1_ref[...] = jnp.zeros_like(st1_ref)
        m1_ref[...] = jnp.zeros_like(m1_ref)

    def step(st_new_ref, m_new_ref, st_old_ref, m_old_ref):
        q = q_ref[...]
        lane = lax.broadcasted_iota(jnp.int32, q.shape, 1)
        zero = jnp.zeros_like(q)
        qq = jnp.concatenate([jnp.where(lane < ATT_QKDIM, q, zero), jnp.where(lane >= ATT_QKDIM, q, zero)], axis=0)
        kc = KEY_CHUNK
        n_chunks = k_ref.shape[0] // kc
        ones = jnp.ones((ONES_ROWS, kc), BF16)
        m_old = m_old_ref[...]
        chunk = lambda c: slice(c * kc, (c + 1) * kc)

        def scores(c, m_new):
            st_c = lax.dot_general(k_ref[chunk(c), :], qq, (((1,), (1,)), ((), ())), preferred_element_type=F32)
            st_new_ref[chunk(c), :] = st_c
            m_c = jnp.max(st_c, axis=0, keepdims=True)
            return m_c if m_new is None else jnp.maximum(m_new, m_c)

        m_new = scores(0, None)
        ovt = None
        for c in range(n_chunks):
            e_c = jnp.exp2(st_old_ref[chunk(c), :] - m_old).astype(BF16)
            if c + 1 < n_chunks:
                m_new = scores(c + 1, m_new)
            vt_c = jnp.concatenate([vt_ref[:, chunk(c)], ones], axis=0)
            o_c = jnp.dot(vt_c, e_c, preferred_element_type=F32)
            ovt = o_c if ovt is None else ovt + o_c
        m_new_ref[...] = m_new
        ovt = ovt[0:ATT_VDIM, :] / ovt[ATT_VDIM:ATT_VDIM + 1, :]
        lv = lv_ref[...]
        lam = (jnp.exp(jnp.sum(lv[0:1, :] * lv[1:2, :], axis=-1, keepdims=True))
               - jnp.exp(jnp.sum(lv[2:3, :] * lv[3:4, :], axis=-1, keepdims=True)) + lam_init)
        ot = ovt[:, 0:tq] - lam * ovt[:, tq:]
        ot = ot * lax.rsqrt(jnp.mean(ot * ot, axis=0, keepdims=True) + EPS) * sg_ref[...] * (1.0 - lam_init)
        o_ref[...] = ot.T.astype(BF16)

    @pl.when(g % 2 == 0)
    def _():
        step(st0_ref, m0_ref, st1_ref, m1_ref)

    @pl.when(g % 2 == 1)
    def _():
        step(st1_ref, m1_ref, st0_ref, m0_ref)


def _attn(qk3, vt3, lam_vec, subln_g, lam_init):
    bsz, s, _ = qk3.shape
    tq = min(MAX_Q_TILE, SCORE_TILE_BYTES // (2 * s * 4))
    n_q = s // tq
    n_tiles = bsz * ATT_HEADS * n_q
    kcol = ATT_WIDTH // LANES
    w2 = lambda g: (0, 0)
    cur = lambda g: jnp.minimum(g, n_tiles - 1)
    prv = lambda g: jnp.maximum(g - 1, 0)
    bat = lambda t: t // (ATT_HEADS * n_q)
    head = lambda t: (t // n_q) % ATT_HEADS
    qi = lambda t: t % n_q
    return pl.pallas_call(
        functools.partial(_attn_kernel, lam_init=lam_init),
        out_shape=jax.ShapeDtypeStruct((bsz, s, ATT_WIDTH), BF16),
        grid=(n_tiles + 1,),
        in_specs=[
            pl.BlockSpec((None, tq, LANES), lambda g: (bat(cur(g)), qi(cur(g)), head(cur(g)))),
            pl.BlockSpec((None, s, LANES), lambda g: (bat(cur(g)), 0, kcol + head(cur(g)))),
            pl.BlockSpec((None, ATT_VDIM, s), lambda g: (bat(prv(g)), head(prv(g)), 0)),
            pl.BlockSpec((4, ATT_QKDIM), w2),
            pl.BlockSpec((ATT_VDIM, 1), w2),
        ],
        out_specs=pl.BlockSpec((None, tq, ATT_VDIM), lambda g: (bat(prv(g)), qi(prv(g)), head(prv(g)))),
        scratch_shapes=[pltpu.VMEM((s, 2 * tq), F32), pltpu.VMEM((s, 2 * tq), F32),
                        pltpu.VMEM((1, 2 * tq), F32), pltpu.VMEM((1, 2 * tq), F32)],
        compiler_params=_cparams("arbitrary"),
        name="attn",
    )(qk3, qk3, vt3, lam_vec, subln_g.reshape(ATT_VDIM, 1))


def _first_argmax(vals):
    best = vals[0]
    idx = jnp.zeros(best.shape, jnp.int32)
    for i in range(1, len(vals)):
        better = vals[i] > best
        idx = jnp.where(better, i, idx)
        best = jnp.where(better, vals[i], best)
    return idx, best


def _out_proj_kernel(x_ref, ya_ref, yb_ref, yc_ref, w_ref, mod_ref, g_ref, rw_ref, rb_ref,
                     xo_ref, hp_ref, meta_ref, cnt_ref):
    tm = TOKEN_TILE
    y = jnp.dot(ya_ref[...], w_ref[0:LRU_WIDTH, :], preferred_element_type=F32)
    y = y + jnp.dot(yb_ref[...], w_ref[LRU_WIDTH:LRU_WIDTH + CONV_WIDTH, :], preferred_element_type=F32)
    y = y + jnp.dot(yc_ref[...], w_ref[LRU_WIDTH + CONV_WIDTH:, :], preferred_element_type=F32)
    x = x_ref[...] + mod_ref[2:3, :] * y
    xo_ref[...] = x
    h = _rms(x, g_ref[...]) * (1.0 + mod_ref[4:5, :]) + mod_ref[3:4, :]
    hp_ref[:, 0:D_MODEL] = h

    logits = lax.dot_general(rw_ref[...], h, (((1,), (1,)), ((), ())), precision=lax.Precision.HIGHEST,
                             preferred_element_type=F32) + rb_ref[...]
    gl = [logits[g:g + 1, :] for g in range(N_GROUPS)]
    g_idx, g_max = _first_argmax(gl)
    denom = gl[0] * 0.0
    for g in range(N_GROUPS):
        denom = denom + jnp.exp(gl[g] - g_max)
    g_w = 1.0 / denom
    fl = []
    for e in range(EXPERTS_PER_GROUP):
        sel = logits[N_GROUPS + e:N_GROUPS + e + 1, :]
        for g in range(1, N_GROUPS):
            r0 = N_GROUPS + g * EXPERTS_PER_GROUP + e
            sel = jnp.where(g_idx == g, logits[r0:r0 + 1, :], sel)
        fl.append(sel)
    i1, v1 = _first_argmax(fl)
    neg = jnp.full(v1.shape, -jnp.inf, F32)
    i2, v2 = _first_argmax([jnp.where(i1 == e, neg, fl[e]) for e in range(EXPERTS_PER_GROUP)])
    t = jnp.exp(v2 - v1)
    p1 = g_w / (1.0 + t)
    p2 = g_w * t / (1.0 + t)
    first_lo = i1 < i2
    lo = jnp.where(first_lo, i1, i2)
    hi = jnp.where(first_lo, i2, i1)
    pair = jnp.where(lo == 0, 0, jnp.where(lo == 1, 3, 5)) + hi - lo - 1
    bucket = g_idx * PAIRS_PER_GROUP + pair
    p_lo = jnp.where(first_lo, p1, p2)
    p_hi = jnp.where(first_lo, p2, p1)
    wrows = jnp.concatenate([p_lo, p_hi, jnp.zeros((LANES - 2, tm), F32)], axis=0)
    hp_ref[:, D_MODEL:] = wrows.T

    @pl.when(pl.program_id(0) == 0)
    def _():
        cnt_ref[...] = jnp.zeros_like(cnt_ref)

    onehot = lax.broadcasted_iota(jnp.int32, (BUCKET_ROWS, tm), 0) == bucket
    before = lax.broadcasted_iota(jnp.int32, (tm, tm), 0) < lax.broadcasted_iota(jnp.int32, (tm, tm), 1)
    prefix = jnp.dot(jnp.where(onehot, 1.0, 0.0).astype(BF16), jnp.where(before, 1.0, 0.0).astype(BF16),
                     preferred_element_type=F32)
    base = cnt_ref[:, 0:1]
    rank = jnp.sum(jnp.where(onehot, prefix + base, 0.0), axis=0, keepdims=True)
    cnt_ref[...] = cnt_ref[...] + jnp.sum(jnp.where(onehot, 1.0, 0.0), axis=1, keepdims=True)
    meta_ref[...] = jnp.concatenate([bucket.astype(F32), rank, jnp.zeros((SUBLANES - 2, tm), F32)], axis=0)


def _out_proj(x, ya, yb, yc, w_out_bf, mod, g2, rw_t, rb, s):
    t = x.shape[0]
    tm = TOKEN_TILE
    per_seq = s // tm
    row = lambda i: (i, 0)
    w2 = lambda i: (0, 0)
    return pl.pallas_call(
        _out_proj_kernel,
        out_shape=(
            jax.ShapeDtypeStruct((t, D_MODEL), F32),
            jax.ShapeDtypeStruct((t, PAYLOAD_COLS), F32),
            jax.ShapeDtypeStruct((SUBLANES, t), F32),
            jax.ShapeDtypeStruct((BUCKET_ROWS, LANES), F32),
        ),
        grid=(t // tm,),
        in_specs=[
            pl.BlockSpec((tm, D_MODEL), row),
            pl.BlockSpec((tm, LRU_WIDTH), row),
            pl.BlockSpec((tm, CONV_WIDTH), row),
            pl.BlockSpec((tm, ATT_WIDTH), row),
            pl.BlockSpec((D_MODEL, D_MODEL), w2),
            pl.BlockSpec((None, 6, D_MODEL), lambda i: (i // per_seq, 0, 0)),
            pl.BlockSpec((1, D_MODEL), w2),
            pl.BlockSpec((ROUTER_ROWS, D_MODEL), w2),
            pl.BlockSpec((ROUTER_ROWS, 1), w2),
        ],
        out_specs=(
            pl.BlockSpec((tm, D_MODEL), row),
            pl.BlockSpec((tm, PAYLOAD_COLS), row),
            pl.BlockSpec((SUBLANES, tm), lambda i: (0, i)),
            pl.BlockSpec((BUCKET_ROWS, LANES), w2),
        ),
        compiler_params=_cparams("arbitrary"),
        name="out_proj",
    )(x, ya, yb, yc, w_out_bf, mod, g2, rw_t, rb)


def _route_tables(meta, cnt, t):
    tm = TOKEN_TILE
    n_tiles = t // tm + N_BUCKETS
    counts = cnt[:N_BUCKETS, 0].astype(jnp.int32)
    tiles_per = (counts + tm - 1) // tm
    tile_end = jnp.cumsum(tiles_per)
    offs = (tile_end - tiles_per) * tm
    n_used = tile_end[-1:]
    bucket = meta[0].astype(jnp.int32)
    rank = meta[1].astype(jnp.int32)
    ids = jnp.arange(N_BUCKETS, dtype=jnp.int32)
    dest = rank + jnp.sum(jnp.where(bucket[:, None] == ids[None, :], offs[None, :], 0), axis=1)
    tile = jnp.minimum(jnp.arange(n_tiles, dtype=jnp.int32), n_used - 1)
    tb = jnp.sum((tile[:, None] >= tile_end[None, :]).astype(jnp.int32), axis=1)
    pair_lo = jnp.array([0, 0, 0, 1, 1, 2], jnp.int32)
    pair_hi = jnp.array([1, 2, 3, 2, 3, 3], jnp.int32)
    onehot_pair = (tb % PAIRS_PER_GROUP)[:, None] == jnp.arange(PAIRS_PER_GROUP, dtype=jnp.int32)[None, :]
    e_lo = (tb // PAIRS_PER_GROUP) * EXPERTS_PER_GROUP + jnp.sum(jnp.where(onehot_pair, pair_lo[None, :], 0), axis=1)
    e_hi = (tb // PAIRS_PER_GROUP) * EXPERTS_PER_GROUP + jnp.sum(jnp.where(onehot_pair, pair_hi[None, :], 0), axis=1)
    last_tile = jnp.where(counts > 0, tile_end - 1, -1)
    idle = n_used + jnp.arange(N_BUCKETS, dtype=jnp.int32)
    zero_tiles = jnp.concatenate([last_tile, jnp.where(idle < n_tiles, idle, -1)])
    return dest, zero_tiles, jnp.stack([e_lo, e_hi], axis=1).reshape(-1), n_used, n_tiles


def _row_copy(src_ref, src_row, dst_ref, dst_row, sem):
    return pltpu.make_async_copy(src_ref.at[pl.ds(src_row, 1), :], dst_ref.at[pl.ds(dst_row, 1), :], sem)


def _issue_rows(n_rows, start_row_copy):
    for r in range(n_rows):
        start_row_copy(r, r % 2)


def _dispatch_kernel(dest_ref, ztile_ref, hp_ref, xs_ref, zbuf_ref, sem, zsem):
    tm = TOKEN_TILE
    base = pl.program_id(0) * tm

    @pl.when(pl.program_id(0) == 0)
    def _():
        zbuf_ref[...] = jnp.zeros_like(zbuf_ref)

        def fill(k, carry):
            @pl.when(ztile_ref[k] >= 0)
            def _():
                row0 = pl.multiple_of(ztile_ref[k] * tm, tm)
                pltpu.make_async_copy(zbuf_ref, xs_ref.at[pl.ds(row0, tm), :], zsem).start()

            return carry

        def drain(k, carry):
            @pl.when(ztile_ref[k] >= 0)
            def _():
                pltpu.make_async_copy(zbuf_ref, xs_ref.at[pl.ds(0, tm), :], zsem).wait()

            return carry

        lax.fori_loop(0, 2 * N_BUCKETS, fill, 0)
        lax.fori_loop(0, 2 * N_BUCKETS, drain, 0)

    _issue_rows(tm, lambda r, pri: _row_copy(hp_ref, r, xs_ref, dest_ref[base + r], sem).start(priority=pri))
    pltpu.make_async_copy(hp_ref, xs_ref.at[pl.ds(0, tm), :], sem).wait()


def _dispatch(dest, zero_tiles, hp, n_rows):
    t = hp.shape[0]
    tm = TOKEN_TILE
    return pl.pallas_call(
        _dispatch_kernel,
        out_shape=jax.ShapeDtypeStruct((n_rows, PAYLOAD_COLS), F32),
        grid_spec=pltpu.PrefetchScalarGridSpec(
            num_scalar_prefetch=2,
            grid=(t // tm,),
            in_specs=[pl.BlockSpec((tm, PAYLOAD_COLS), lambda i, dest, zt: (i, 0))],
            out_specs=pl.BlockSpec(memory_space=pl.ANY),
            scratch_shapes=[pltpu.VMEM((tm, PAYLOAD_COLS), F32), pltpu.SemaphoreType.DMA, pltpu.SemaphoreType.DMA],
        ),
        compiler_params=_cparams("arbitrary"),
        name="dispatch",
    )(dest, zero_tiles, hp)


def _experts_kernel(eid_ref, nused_ref, x_ref, w1a_ref, w3a_ref, w2a_ref, w1b_ref, w3b_ref, w2b_ref, o_ref,
                    w1a_bf, w3a_bf, w2a_bf, w1b_bf, w3b_bf, w2b_bf):
    i = pl.program_id(0)
    busy = i < nused_ref[0]
    prev = jnp.maximum(i - 1, 0)

    def refresh(slot, pairs):
        @pl.when(jnp.logical_or(i == 0, eid_ref[2 * i + slot] != eid_ref[2 * prev + slot]))
        def _():
            for w_ref, w_bf in pairs:
                w_bf[...] = w_ref[...].astype(BF16)

    refresh(0, ((w1a_ref, w1a_bf), (w3a_ref, w3a_bf), (w2a_ref, w2a_bf)))
    refresh(1, ((w1b_ref, w1b_bf), (w3b_ref, w3b_bf), (w2b_ref, w2b_bf)))

    @pl.when(jnp.logical_not(busy))
    def _():
        o_ref[...] = jnp.zeros_like(o_ref)

    @pl.when(busy)
    def _():
        x = x_ref[:, 0:D_MODEL].astype(BF16)

        def expert(w1_bf, w3_bf, w2_bf):
            a = jnp.dot(x, w1_bf[...], preferred_element_type=F32)
            he = (a * jax.nn.sigmoid(a)) * jnp.dot(x, w3_bf[...], preferred_element_type=F32)
            return jnp.dot(he.astype(BF16), w2_bf[...], preferred_element_type=F32)

        o_ref[...] = (x_ref[:, D_MODEL:D_MODEL + 1] * expert(w1a_bf, w3a_bf, w2a_bf)
                      + x_ref[:, D_MODEL + 1:D_MODEL + 2] * expert(w1b_bf, w3b_bf, w2b_bf))


def _experts(eids, n_used, xs, w1, w3, w2, layer):
    n_rows = xs.shape[0]
    tm = TOKEN_TILE
    rows = lambda i, eid, nu: (jnp.maximum(jnp.minimum(i, nu[0] - 1), 0), 0)
    wspec = lambda shape, slot: pl.BlockSpec((None, None) + shape,
                                             lambda i, eid, nu: (layer, eid[2 * i + slot], 0, 0))
    up, down = (D_MODEL, D_EXPERT), (D_EXPERT, D_MODEL)
    return pl.pallas_call(
        _experts_kernel,
        out_shape=jax.ShapeDtypeStruct((n_rows, D_MODEL), F32),
        grid_spec=pltpu.PrefetchScalarGridSpec(
            num_scalar_prefetch=2,
            grid=(n_rows // tm,),
            in_specs=[
                pl.BlockSpec((tm, PAYLOAD_COLS), rows),
                wspec(up, 0), wspec(up, 0), wspec(down, 0),
                wspec(up, 1), wspec(up, 1), wspec(down, 1),
            ],
            out_specs=pl.BlockSpec((tm, D_MODEL), lambda i, eid, nu: (i, 0)),
            scratch_shapes=[pltpu.VMEM(up, BF16), pltpu.VMEM(up, BF16), pltpu.VMEM(down, BF16),
                            pltpu.VMEM(up, BF16), pltpu.VMEM(up, BF16), pltpu.VMEM(down, BF16)],
        ),
        compiler_params=_cparams("arbitrary"),
        name="experts",
    )(eids, n_used, xs, w1, w3, w2, w1, w3, w2)


def _combine_kernel(dest_ref, x_ref, ys_ref, mod_ref, g_ref, o_ref, ybuf_ref, sem, *, final):
    tm = TOKEN_TILE
    base = pl.program_id(0) * tm

    _issue_rows(tm, lambda r, pri: _row_copy(ys_ref, dest_ref[base + r], ybuf_ref, r, sem).start(priority=pri))
    pltpu.make_async_copy(ys_ref.at[pl.ds(0, tm), :], ybuf_ref, sem).wait()
    x = x_ref[...] + mod_ref[5:6, :] * ybuf_ref[...]
    o_ref[...] = _rms(x, g_ref[...]) if final else x


def _combine(dest, x, ys, mod, final_g, s, final):
    t = x.shape[0]
    tm = TOKEN_TILE
    per_seq = s // tm
    return pl.pallas_call(
        functools.partial(_combine_kernel, final=final),
        out_shape=jax.ShapeDtypeStruct((t, D_MODEL), F32),
        grid_spec=pltpu.PrefetchScalarGridSpec(
            num_scalar_prefetch=1,
            grid=(t // tm,),
            in_specs=[
                pl.BlockSpec((tm, D_MODEL), lambda i, dest: (i, 0)),
                pl.BlockSpec(memory_space=pl.ANY),
                pl.BlockSpec((None, 6, D_MODEL), lambda i, dest: (i // per_seq, 0, 0)),
                pl.BlockSpec((1, D_MODEL), lambda i, dest: (0, 0)),
            ],
            out_specs=pl.BlockSpec((tm, D_MODEL), lambda i, dest: (i, 0)),
            scratch_shapes=[pltpu.VMEM((tm, D_MODEL), F32), pltpu.SemaphoreType.DMA],
        ),
        compiler_params=_cparams("arbitrary"),
        name="combine",
    )(dest, x, ys, mod, final_g)


def _block_diag(w):
    h, dh, _ = w.shape
    eye = jnp.eye(h, dtype=w.dtype)
    return (eye[:, None, :, None] * w[:, :, None, :]).reshape(h * dh, h * dh)


def _prep_layer(l, p):
    row = lambda a: a.reshape(1, -1)
    rw = jnp.concatenate([p["router_w1"][l], p["router_w2"][l].reshape(D_MODEL, N_EXPERTS)], axis=1)
    rw = jnp.pad(rw, ((0, 0), (0, ROUTER_ROWS - rw.shape[1])))
    rb = jnp.concatenate([p["router_b1"][l], p["router_b2"][l].reshape(N_EXPERTS)])
    rb = jnp.pad(rb, (0, ROUTER_ROWS - rb.shape[0]))
    return dict(
        norm1_g=row(p["norm1_g"][l]),
        norm2_g=row(p["norm2_g"][l]),
        w_in=p["w_in"][l].astype(BF16),
        w_out=p["w_out"][l].astype(BF16),
        conv_w=p["conv_w"][l],
        conv_b=row(p["conv_b"][l]),
        wa_bd=jnp.stack([_block_diag(p["lru_wa"][l, d]) for d in range(2)]).astype(BF16),
        wx_bd=jnp.stack([_block_diag(p["lru_wx"][l, d]) for d in range(2)]).astype(BF16),
        ba=p["lru_ba"][l].reshape(2, 1, LRU_WIDTH),
        bx=p["lru_bx"][l].reshape(2, 1, LRU_WIDTH),
        lam=p["lru_lambda"][l].reshape(2, 1, LRU_WIDTH),
        dw_w=p["dw_w"][l],
        dw_b=row(p["dw_b"][l]),
        cln_g=row(p["cln_g"][l]),
        cln_b=row(p["cln_b"][l]),
        lam_vec=p["lam_vec"][l],
        subln_g=row(p["subln_g"][l]),
        rw_t=rw.T,
        rb=rb.reshape(ROUTER_ROWS, 1),
    )


def _trunk(x, mods, layers, moe, final_g):
    bsz, s, _ = x.shape
    t = bsz * s
    tables = _rope_tables(s)
    x = x.reshape(t, D_MODEL)
    for l, w in enumerate(layers):
        mod = mods[l]
        ab, qk, vt = _in_proj(x, mod, w["norm1_g"], w["w_in"], tables, s)
        ab3 = ab.reshape(bsz, s, AB_COLS)
        ya = _lru(ab3, w["conv_w"], w["conv_b"], w["wa_bd"], w["ba"], w["wx_bd"], w["bx"], w["lam"])
        yb = _convmod(ab3, w["dw_w"], w["dw_b"], w["cln_g"], w["cln_b"])
        lam_init = 0.8 - 0.6 * math.exp(-0.3 * l)
        yc = _attn(qk.reshape(bsz, s, QK_COLS), vt, w["lam_vec"], w["subln_g"], lam_init)
        x, hp, meta, cnt = _out_proj(x, ya.reshape(t, LRU_WIDTH), yb.reshape(t, CONV_WIDTH),
                                     yc.reshape(t, ATT_WIDTH), w["w_out"], mod, w["norm2_g"], w["rw_t"], w["rb"], s)
        dest, zero_tiles, eids, n_used, n_tiles = _route_tables(meta, cnt, t)
        xs = _dispatch(dest, zero_tiles, hp, n_tiles * TOKEN_TILE)
        ys = _experts(eids, n_used, xs, moe["moe_w1"], moe["moe_w3"], moe["moe_w2"], l)
        x = _combine(dest, x, ys, mod, final_g, s, final=(l == DEPTH - 1))
    return x.reshape(bsz, s, D_MODEL)


def kernel(x_prompt, x_sample, c_prompt, c_sample, norm1_g, norm2_g, final_g, ada_w, ada_b, w_in, w_out, conv_w, conv_b, lru_wa, lru_ba, lru_wx, lru_bx, lru_lambda, dw_w, dw_b, cln_g, cln_b, lam_vec, subln_g, router_w1, router_b1, router_w2, router_b2, moe_w1, moe_w3, moe_w2):
    p = dict(norm1_g=norm1_g, norm2_g=norm2_g, w_in=w_in, w_out=w_out, conv_w=conv_w, conv_b=conv_b,
             lru_wa=lru_wa, lru_ba=lru_ba, lru_wx=lru_wx, lru_bx=lru_bx, lru_lambda=lru_lambda,
             dw_w=dw_w, dw_b=dw_b, cln_g=cln_g, cln_b=cln_b, lam_vec=lam_vec, subln_g=subln_g,
             router_w1=router_w1, router_b1=router_b1, router_w2=router_w2, router_b2=router_b2,
             moe_w1=moe_w1, moe_w3=moe_w3, moe_w2=moe_w2)
    layers = [_prep_layer(l, p) for l in range(DEPTH)]
    nb = c_prompt.shape[0]
    mods = _ada_mod(jnp.concatenate([c_prompt, c_sample], axis=0), ada_w, ada_b)
    fg = final_g.reshape(1, D_MODEL)
    moe = dict(moe_w1=moe_w1, moe_w3=moe_w3, moe_w2=moe_w2)
    y_prompt = _trunk(x_prompt, mods[:, :nb], layers, moe, fg)
    y_sample = _trunk(x_sample, mods[:, nb:], layers, moe, fg)
    return (y_prompt, y_sample)
```
